```python
import jax, jax.numpy as jnp
from jax import lax
import numpy as np

D_MODEL = 2048
BATCH = 2
SEQ = 8192
DEPTH = 1

CHUNK = 64
MIX_WIDTH = D_MODEL
HEAD_DIM = 64
RWKV_WIDTH = MIX_WIDTH // 2
ATT_WIDTH = MIX_WIDTH - RWKV_WIDTH
RWKV_HEADS = RWKV_WIDTH // HEAD_DIM
ATT_HEADS = ATT_WIDTH // HEAD_DIM
DECAY_LORA = 64
AAA_LORA = 64
GATE_LORA = 128
RWKV_COLS = 3 * RWKV_WIDTH + DECAY_LORA + AAA_LORA + GATE_LORA
IN_COLS = RWKV_COLS + 3 * ATT_WIDTH
LEFT_CHUNKS = 8
BAND = (LEFT_CHUNKS + 1) * CHUNK
REL_CLIP = 128
PEER_HEADS = 8
N_KEYS = 128
N_EXPERTS = N_KEYS * N_KEYS
PEER_QDIM = 256
PEER_HALF = PEER_QDIM // 2
PEER_TOPK = 16
PEER_TOKEN_BLOCK = 128
PLE_DIM = 256
NORM_EPS = 1e-6
LNX_EPS = 64e-5

kernel_name = "hybrid_rwkv7_chunkattn_peer_block"


def rms_norm(x, g):
    x32 = x.astype(jnp.float32)
    y = x32 * lax.rsqrt(jnp.mean(x32 * x32, axis=-1, keepdims=True) + NORM_EPS)
    return (y * g.astype(jnp.float32)).astype(x.dtype)


def rwkv7_mixer(z, mu, decay_up, decay_w0, iclr_up, iclr_a0, gate_up, k_k, k_a, r_k, lnx_g, lnx_b):
    B, T, _ = z.shape
    H, N = RWKV_HEADS, HEAD_DIM
    dt = z.dtype
    z_prev = jnp.pad(z[:, :-1], ((0, 0), (1, 0), (0, 0)))
    z = z + (z_prev - z) * mu
    splits = [RWKV_WIDTH, 2 * RWKV_WIDTH, 3 * RWKV_WIDTH,
              3 * RWKV_WIDTH + DECAY_LORA, 3 * RWKV_WIDTH + DECAY_LORA + AAA_LORA]
    r, k, v, wd, ad, gd = jnp.split(z, splits, axis=-1)
    w_log = -jax.nn.softplus(-(decay_w0 + jnp.tanh(wd) @ decay_up).astype(jnp.float32)) - 0.5
    w = jnp.exp(-jnp.exp(w_log)).astype(dt)
    a = jax.nn.sigmoid(iclr_a0 + ad @ iclr_up)
    g = jax.nn.sigmoid(gd) @ gate_up
    kk = (k * k_k).reshape(B, T, H, N).astype(jnp.float32)
    kk = (kk / jnp.maximum(jnp.sqrt(jnp.sum(kk * kk, -1, keepdims=True)), 1e-12)).astype(dt)
    k = k * (1.0 + (a - 1.0) * k_a)
    r_h, w_h, k_h, v_h, a_h = (t.reshape(B, T, H, N) for t in (r, w, k, v, a))

    def step(S, inp):
        r_t, w_t, k_t, v_t, kk_t, a_t = inp
        s_kk = jnp.einsum('bhvk,bhk->bhv', S, -kk_t)
        S = (S * w_t[:, :, None, :]
             + s_kk[..., None] * (kk_t * a_t)[:, :, None, :]
             + v_t[..., None] * k_t[:, :, None, :])
        return S, jnp.einsum('bhvk,bhk->bhv', S, r_t)

    tfirst = lambda t: jnp.swapaxes(t, 0, 1)
    S0 = jnp.zeros((B, H, N, N), dt)
    _, y = lax.scan(step, S0, tuple(tfirst(t) for t in (r_h, w_h, k_h, v_h, kk, a_h)))
    y = tfirst(y).astype(jnp.float32)
    mean = jnp.mean(y, -1, keepdims=True)
    var = jnp.mean(jnp.square(y - mean), -1, keepdims=True)
    y = ((y - mean) * lax.rsqrt(var + LNX_EPS)).reshape(B, T, RWKV_WIDTH)
    y = (y * lnx_g.astype(jnp.float32) + lnx_b.astype(jnp.float32)).astype(dt)
    bonus = (jnp.sum(r_h * k_h * r_k, -1, keepdims=True) * v_h).reshape(B, T, RWKV_WIDTH)
    return (y + bonus) * g


def chunk_attention(q, k, v, rel_bias):
    B, T, _ = q.shape
    H, Dh = ATT_HEADS, HEAD_DIM
    n_chunks = T // CHUNK
    pad = LEFT_CHUNKS * CHUNK
    heads = lambda t: t.reshape(B, T, H, Dh).transpose(0, 2, 1, 3)
    q = heads(q) * (Dh ** -0.5)
    k = jnp.pad(heads(k), ((0, 0), (0, 0), (pad, 0), (0, 0)))
    v = jnp.pad(heads(v), ((0, 0), (0, 0), (pad, 0), (0, 0)))
    qi = jnp.arange(CHUNK)[:, None] + pad
    kj = jnp.arange(BAND)[None, :]
    rel = jnp.clip(qi - kj, -REL_CLIP, REL_CLIP) + REL_CLIP
    bias = rel_bias[:, rel].astype(jnp.float32)

    def one_chunk(c):
        start = c * CHUNK
        q_c = lax.dynamic_slice_in_dim(q, start, CHUNK, axis=2)
        k_c = lax.dynamic_slice_in_dim(k, start, BAND, axis=2)
        v_c = lax.dynamic_slice_in_dim(v, start, BAND, axis=2)
        s = jnp.einsum('bhqd,bhkd->bhqk', q_c, k_c).astype(jnp.float32) + bias
        s = jnp.where(kj >= pad - start, s, -jnp.inf)
        pr = jax.nn.softmax(s, axis=-1).astype(v_c.dtype)
        return jnp.einsum('bhqk,bhkd->bhqd', pr, v_c)

    o = lax.map(one_chunk, jnp.arange(n_chunks))
    return o.transpose(1, 0, 3, 2, 4).reshape(B, T, H * Dh)


def peer_ffn(x, wq, sub_keys, u, v):
    B, T, D = x.shape
    M = B * T
    xf = x.reshape(M, D)
    q = (xf @ wq).reshape(M, PEER_HEADS, 2, PEER_HALF)
    s = jnp.einsum('mhid,hind->mhin', q, sub_keys).astype(jnp.float32)
    s_top, i_top = lax.top_k(s, PEER_TOPK)
    cand = s_top[:, :, 0, :, None] + s_top[:, :, 1, None, :]
    cand_idx = i_top[:, :, 0, :, None] * N_KEYS + i_top[:, :, 1, None, :]
    best, pos = lax.top_k(cand.reshape(M, PEER_HEADS, -1), PEER_TOPK)
    experts = jnp.take_along_axis(cand_idx.reshape(M, PEER_HEADS, -1), pos, axis=-1)
    gates = jax.nn.softmax(best, axis=-1).astype(x.dtype)
    n_blocks = M // PEER_TOKEN_BLOCK

    def block(args):
        xb, eb, gb = args
        act = jax.nn.gelu(jnp.einsum('tkd,td->tk', u[eb], xb))
        return jnp.einsum('tk,tkd->td', gb * act, v[eb])

    out = lax.map(block, (xf.reshape(n_blocks, PEER_TOKEN_BLOCK, D),
                          experts.reshape(n_blocks, PEER_TOKEN_BLOCK, -1),
                          gates.reshape(n_blocks, PEER_TOKEN_BLOCK, -1)))
    return out.reshape(B, T, D)


def setup_inputs(seed: int = 0) -> dict:
    key = jax.random.key(seed)
    ks = iter(jax.random.split(key, 40))

    def nrm(shape, scale):
        return jax.random.normal(next(ks), shape, jnp.float32) * scale

    def gain(shape):
        return 1.0 + nrm(shape, 0.02)

    L, D = DEPTH, D_MODEL
    return {
        "x": nrm((BATCH, SEQ, D), 1.0),
        "p": nrm((DEPTH, BATCH, SEQ, PLE_DIM), 1.0),
        "mix_norm_g": gain((L, D)),
        "w_in": nrm((L, D, IN_COLS), D ** -0.5),
        "rwkv_mu": jax.random.uniform(next(ks), (L, RWKV_COLS), jnp.float32),
        "decay_up": nrm((L, DECAY_LORA, RWKV_WIDTH), 0.1 * DECAY_LORA ** -0.5),
        "decay_w0": jax.random.uniform(next(ks), (L, RWKV_WIDTH), jnp.float32, -4.0, 1.0),
        "iclr_up": nrm((L, AAA_LORA, RWKV_WIDTH), 0.1 * AAA_LORA ** -0.5),
        "iclr_a0": nrm((L, RWKV_WIDTH), 0.1),
        "gate_up": nrm((L, GATE_LORA, RWKV_WIDTH), GATE_LORA ** -0.5),
        "k_k": 0.85 + nrm((L, RWKV_WIDTH), 0.02),
        "k_a": gain((L, RWKV_WIDTH)),
        "r_k": nrm((L, RWKV_HEADS, HEAD_DIM), 0.1),
        "lnx_g": gain((L, RWKV_WIDTH)),
        "lnx_b": nrm((L, RWKV_WIDTH), 0.01),
        "rel_bias": nrm((L, ATT_HEADS, 2 * REL_CLIP + 1), 0.1),
        "w_out": nrm((L, MIX_WIDTH, D), MIX_WIDTH ** -0.5),
        "ffn_norm_g": gain((L, D)),
        "peer_wq": nrm((L, D, PEER_HEADS * PEER_QDIM), D ** -0.5),
        "peer_sub_keys": nrm((L, PEER_HEADS, 2, N_KEYS, PEER_HALF), PEER_HALF ** -0.5),
        "peer_u": nrm((L, N_EXPERTS, D), D ** -0.5),
        "peer_v": nrm((L, N_EXPERTS, D), 0.2),
        "ple_norm_g": gain((L, D)),
        "ple_gate_w": nrm((L, D, D), D ** -0.5),
        "ple_up": nrm((L, PLE_DIM, D), PLE_DIM ** -0.5),
        "final_norm_g": gain((D,)),
    }


def reference(x, p, mix_norm_g, w_in, rwkv_mu, decay_up, decay_w0, iclr_up, iclr_a0,
              gate_up, k_k, k_a, r_k, lnx_g, lnx_b, rel_bias, w_out, ffn_norm_g,
              peer_wq, peer_sub_keys, peer_u, peer_v, ple_norm_g, ple_gate_w, ple_up,
              final_norm_g):
    h = x
    for i in range(DEPTH):
        z = rms_norm(h, mix_norm_g[i]) @ w_in[i]
        z_rwkv, z_att = z[..., :RWKV_COLS], z[..., RWKV_COLS:]
        y_rwkv = rwkv7_mixer(z_rwkv, rwkv_mu[i], decay_up[i], decay_w0[i], iclr_up[i],
                             iclr_a0[i], gate_up[i], k_k[i], k_a[i], r_k[i], lnx_g[i], lnx_b[i])
        q_a, k_a_, v_a = jnp.split(z_att, 3, axis=-1)
        y_att = chunk_attention(q_a, k_a_, v_a, rel_bias[i])
        h = h + jnp.concatenate([y_rwkv, y_att], axis=-1) @ w_out[i]
        h = h + peer_ffn(rms_norm(h, ffn_norm_g[i]), peer_wq[i], peer_sub_keys[i], peer_u[i], peer_v[i])
        gate = jax.nn.sigmoid(rms_norm(h, ple_norm_g[i]) @ ple_gate_w[i])
        h = h + gate * (p[i] @ ple_up[i])
    return rms_norm(h, final_norm_g)
```

```python
import functools

import jax
import jax.numpy as jnp
from jax import lax
from jax.experimental import pallas as pl
from jax.experimental.pallas import tpu as pltpu

F32 = jnp.float32
BF16 = jnp.bfloat16

HEAD_DIM = 64
CHUNK = 64
LEFT_CHUNKS = 8
REL_CLIP = 128
DECAY_LORA = 64
AAA_LORA = 64
GATE_LORA = 128
PEER_HEADS = 8
N_KEYS = 128
PEER_HALF = 128
PEER_TOPK = 16
NORM_EPS = 1e-6
LNX_EPS = 64e-5

VMEM_LIMIT_BYTES = 56 * 1024 * 1024
GROUP_HEADS = 4
GROUP_W = GROUP_HEADS * HEAD_DIM


def _cparams(sem):
    return pltpu.CompilerParams(dimension_semantics=sem, vmem_limit_bytes=VMEM_LIMIT_BYTES)


def _dot(a, b):
    return jnp.dot(a, b, preferred_element_type=F32)


def _dot_nt(a, b):
    return lax.dot_general(a, b, (((1,), (1,)), ((), ())), preferred_element_type=F32)


def _dot_tn(a, b):
    return lax.dot_general(a, b, (((0,), (0,)), ((), ())), preferred_element_type=F32)


def _rms(x, g):
    return x * lax.rsqrt(jnp.mean(x * x, axis=-1, keepdims=True) + NORM_EPS) * g


def _sigmoid(x):
    return 1.0 / (1.0 + jnp.exp(-x))


def _norm_matmul_kernel(x_ref, g_ref, w_ref, o_ref, xn_ref):
    @pl.when(pl.program_id(1) == 0)
    def _():
        xn_ref[...] = _rms(x_ref[...], g_ref[...]).astype(BF16)

    o_ref[...] = _dot(xn_ref[...], w_ref[...]).astype(o_ref.dtype)


def _norm_matmul(x, g, w, out_dtype, tm, tn):
    m, d = x.shape
    n = w.shape[1]
    return pl.pallas_call(
        _norm_matmul_kernel,
        grid=(m // tm, n // tn),
        in_specs=[
            pl.BlockSpec((tm, d), lambda i, j: (i, 0)),
            pl.BlockSpec((1, d), lambda i, j: (0, 0)),
            pl.BlockSpec((d, tn), lambda i, j: (0, j)),
        ],
        out_specs=pl.BlockSpec((tm, tn), lambda i, j: (i, j)),
        out_shape=jax.ShapeDtypeStruct((m, n), out_dtype),
        scratch_shapes=[pltpu.VMEM((tm, d), BF16)],
        compiler_params=_cparams(("parallel", "arbitrary")),
        name="norm_matmul",
    )(x, g.reshape(1, d), w)


def _split_dot(x, w_bf16):
    hi = x.astype(BF16)
    lo = (x - hi.astype(F32)).astype(BF16)
    return _dot(hi, w_bf16) + _dot(lo, w_bf16)


def _rwkv_kernel(zr_ref, zk_ref, zv_ref, zl_ref, mur_ref, muk_ref, muv_ref, mul_ref,
                 dup_ref, dw0_ref, iup_ref, ia0_ref, gup_ref, kk_ref, ka_ref, rk_ref,
                 lg_ref, lb_ref, y_ref, state_ref, last_ref, *, tm):
    j = pl.program_id(2)
    n_chunks = tm // CHUNK

    @pl.when(j == 0)
    def _():
        state_ref[...] = jnp.zeros_like(state_ref)
        last_ref[...] = jnp.zeros_like(last_ref)

    row = lax.broadcasted_iota(jnp.int32, (tm, GROUP_W), 0)

    def shift_mix(ref, mu_ref, slot):
        cur = ref[...]
        prev = pltpu.roll(cur, 1, axis=0)
        prev = jnp.where(row == 0, last_ref[slot:slot + 1, :], prev)
        last_ref[slot:slot + 1, :] = cur[tm - 1:tm, :]
        return cur + (prev - cur) * mu_ref[...]

    r = shift_mix(zr_ref, mur_ref, 0)
    k = shift_mix(zk_ref, muk_ref, 1)
    v = shift_mix(zv_ref, muv_ref, 2)
    lo = shift_mix(zl_ref, mul_ref, 3)
    wd = lo[:, :DECAY_LORA]
    ad = lo[:, DECAY_LORA:DECAY_LORA + AAA_LORA]
    gd = lo[:, DECAY_LORA + AAA_LORA:]

    dec = dw0_ref[...] + _dot(jnp.tanh(wd).astype(BF16), dup_ref[...])
    neg = -dec
    softplus = jnp.maximum(neg, 0.0) + jnp.log(1.0 + jnp.exp(-jnp.abs(neg)))
    logw = -jnp.exp(-softplus - 0.5)
    a = _sigmoid(ia0_ref[...] + _dot(ad.astype(BF16), iup_ref[...]))
    g = _dot(_sigmoid(gd).astype(BF16), gup_ref[...])

    hrow = lax.broadcasted_iota(jnp.int32, (GROUP_W, GROUP_W), 0)
    hcol = lax.broadcasted_iota(jnp.int32, (GROUP_W, GROUP_W), 1)
    same = (hrow // HEAD_DIM) == (hcol // HEAD_DIM)
    ones_bd = jnp.where(same, 1.0, 0.0).astype(BF16)

    kkr = k * kk_ref[...]
    kk_norm = jnp.sqrt(_split_dot(kkr * kkr, ones_bd))
    kk = kkr / jnp.maximum(kk_norm, 1e-12)
    k2 = k * (1.0 + (a - 1.0) * ka_ref[...])
    bonus = _split_dot(r * k2 * rk_ref[...], ones_bd) * v
    avec = -kk
    bvec = kk * a

    t_row = hrow % CHUNK
    t_col = hcol % CHUNK
    strict = same & (t_row > t_col)
    incl = same & (t_row >= t_col)
    eye = hrow == hcol
    tri = jnp.where(lax.broadcasted_iota(jnp.int32, (CHUNK, CHUNK), 0)
                    >= lax.broadcasted_iota(jnp.int32, (CHUNK, CHUNK), 1), 1.0, 0.0).astype(BF16)

    def tile4(x):
        return jnp.concatenate([x] * GROUP_HEADS, axis=0)

    def bd(x):
        return jnp.where(same, tile4(x), 0.0)

    ys = []
    state = state_ref[...]
    for c in range(n_chunks):
        sl = slice(c * CHUNK, (c + 1) * CHUNK)
        lw = logw[sl]
        lw_hi = lw.astype(BF16)
        rem = lw - lw_hi.astype(F32)
        lw_mid = rem.astype(BF16)
        lw_lo = (rem - lw_mid.astype(F32)).astype(BF16)
        cum = _dot(tri, lw_hi) + _dot(tri, lw_mid) + _dot(tri, lw_lo)
        e_prev = jnp.exp(cum - lw)
        e_cum = jnp.exp(cum)
        e_inv = jnp.exp(-cum)
        e_end = jnp.exp(cum[CHUNK - 1:CHUNK, :] - cum)
        a_hat = avec[sl] * e_prev
        r_hat = r[sl] * e_cum
        b_hat = bvec[sl] * e_inv
        k_hat = k2[sl] * e_inv
        b_end = bvec[sl] * e_end
        k_end = k2[sl] * e_end
        p_end = e_cum[CHUNK - 1:CHUNK, :]

        bd_a = bd(a_hat).astype(BF16)
        bd_r = bd(r_hat).astype(BF16)
        bd_v = bd(v[sl]).astype(BF16)
        lhs = jnp.concatenate([bd_a, bd_r], axis=0)
        rhs = jnp.concatenate([tile4(b_hat), tile4(k_hat)], axis=0).astype(BF16)
        lm = _dot_nt(lhs, rhs)
        l_ab = jnp.where(strict, lm[:GROUP_W, :GROUP_W], 0.0)
        l_ak = jnp.where(strict, lm[:GROUP_W, GROUP_W:], 0.0)
        m_rb = jnp.where(incl, lm[GROUP_W:, :GROUP_W], 0.0).astype(BF16)
        m_rk = jnp.where(incl, lm[GROUP_W:, GROUP_W:], 0.0).astype(BF16)

        t_inv = jnp.where(eye, 1.0, 0.0) + l_ab
        l_pow = l_ab
        for _ in range(5):
            lp16 = l_pow.astype(BF16)
            l_pow = _dot(lp16, lp16)
            t_inv = t_inv + _dot(t_inv.astype(BF16), l_pow.astype(BF16))

        lakv = _dot(l_ak.astype(BF16), bd_v)
        tw = _dot(t_inv.astype(BF16),
                  jnp.concatenate([bd_a, lakv.astype(BF16)], axis=1))
        tw16 = tw.astype(BF16)
        x1 = _dot_tn(bd(b_end).astype(BF16), tw16)
        x2 = _dot_tn(bd(k_end).astype(BF16), bd_v)
        a_c = jnp.where(eye, p_end, 0.0) + x1[:, :GROUP_W]
        c_c = x1[:, GROUP_W:] + x2
        mt = _dot(m_rb, tw16)
        q_c = bd(r_hat) + mt[:, :GROUP_W]
        y_loc = mt[:, GROUP_W:] + _dot(m_rk, bd_v)

        s16 = state.astype(BF16)
        y_bd = _dot(q_c.astype(BF16), s16) + y_loc
        state = _dot(a_c.astype(BF16), s16) + c_c
        y_w = y_bd[0:CHUNK]
        for h in range(1, GROUP_HEADS):
            y_w = y_w + y_bd[h * CHUNK:(h + 1) * CHUNK]
        ys.append(y_w)
    state_ref[...] = state

    y = jnp.concatenate(ys, axis=0)
    mean = _split_dot(y, ones_bd) * (1.0 / HEAD_DIM)
    yc = y - mean
    var = _split_dot(yc * yc, ones_bd) * (1.0 / HEAD_DIM)
    yn = yc * lax.rsqrt(var + LNX_EPS) * lg_ref[...] + lb_ref[...]
    y_ref[...] = ((yn + bonus) * g).astype(y_ref.dtype)


def _rwkv(z, batch, seq, mu, decay_up, decay_w0, iclr_up, iclr_a0, gate_up, k_k, k_a, r_k,
          lnx_g, lnx_b, tm):
    m = z.shape[0]
    width = decay_up.shape[1]
    groups = width // GROUP_W
    nt = seq // tm
    lora_blk = 3 * groups

    def zspec(off):
        return pl.BlockSpec((tm, GROUP_W), lambda b, g, j: (b * nt + j, off(g)))

    def muspec(off):
        return pl.BlockSpec((1, GROUP_W), lambda b, g, j: (0, off(g)))

    def pspec(rows):
        return pl.BlockSpec((rows, GROUP_W), lambda b, g, j: (0, g))

    offs = [lambda g: g, lambda g: groups + g, lambda g: 2 * groups + g, lambda g: lora_blk]
    vec = lambda t: t.reshape(1, width)
    mu2 = mu.reshape(1, -1)
    return pl.pallas_call(
        functools.partial(_rwkv_kernel, tm=tm),
        grid=(batch, groups, nt),
        in_specs=[zspec(o) for o in offs] + [muspec(o) for o in offs] + [
            pspec(DECAY_LORA), pspec(1), pspec(AAA_LORA), pspec(1), pspec(GATE_LORA),
            pspec(1), pspec(1), pspec(1), pspec(1), pspec(1)],
        out_specs=pl.BlockSpec((tm, GROUP_W), lambda b, g, j: (b * nt + j, g)),
        out_shape=jax.ShapeDtypeStruct((m, width), BF16),
        scratch_shapes=[pltpu.VMEM((GROUP_W, GROUP_W), F32), pltpu.VMEM((8, GROUP_W), F32)],
        compiler_params=_cparams(("parallel", "parallel", "arbitrary")),
        name="rwkv7",
    )(z, z, z, z, mu2, mu2, mu2, mu2,
      decay_up.astype(BF16), vec(decay_w0), iclr_up.astype(BF16), vec(iclr_a0),
      gate_up.astype(BF16), vec(k_k), vec(k_a), vec(r_k), vec(lnx_g), vec(lnx_b))


def _attn_kernel(q_ref, kp_ref, kc_ref, vp_ref, vc_ref, bias_ref, o_ref, *, tq):
    j = pl.program_id(2)
    q = q_ref[...]
    kcat = jnp.concatenate([kp_ref[...], kc_ref[...]], axis=0)
    vcat = jnp.concatenate([vp_ref[...], vc_ref[...]], axis=0)
    lane = lax.broadcasted_iota(jnp.int32, (tq, 2 * HEAD_DIM), 1)
    col = lax.broadcasted_iota(jnp.int32, (tq, 2 * tq), 1)
    in_seq = (col >= tq) | (j > 0)
    outs = []
    for hh in range(2):
        mine = (lane // HEAD_DIM) == hh
        qm = jnp.where(mine, q, jnp.zeros_like(q)) * (HEAD_DIM ** -0.5)
        s = _dot_nt(qm.astype(BF16), kcat) + bias_ref[hh]
        s = jnp.where(in_seq, s, -jnp.inf)
        mx = jnp.max(s, axis=-1, keepdims=True)
        p = jnp.exp(s - mx)
        den = jnp.sum(p, axis=-1, keepdims=True)
        outs.append(_dot(p.astype(BF16), vcat) / den)
    o_ref[...] = jnp.where((lane // HEAD_DIM) == 0, outs[0], outs[1]).astype(o_ref.dtype)


def _attention(z_att, batch, seq, rel_bias, tq):
    m, w3 = z_att.shape
    width = w3 // 3
    heads = width // HEAD_DIM
    pairs = heads // 2
    nt = seq // tq
    qi = jnp.arange(tq)[:, None]
    kj = jnp.arange(2 * tq)[None, :] - tq
    rel = jnp.clip(qi - kj, -REL_CLIP, REL_CLIP) + REL_CLIP
    qc = qi // CHUNK
    kc = jnp.floor_divide(kj, CHUNK)
    band = (kc <= qc) & (kc >= qc - LEFT_CHUNKS)
    bias = jnp.where(band[None], rel_bias[:, rel].astype(F32), -jnp.inf)

    def spec(col_off, prev):
        if prev:
            return pl.BlockSpec((tq, 2 * HEAD_DIM),
                                lambda b, hp, j: (b * nt + jnp.maximum(j - 1, 0), col_off + hp))
        return pl.BlockSpec((tq, 2 * HEAD_DIM), lambda b, hp, j: (b * nt + j, col_off + hp))

    return pl.pallas_call(
        functools.partial(_attn_kernel, tq=tq),
        grid=(batch, pairs, nt),
        in_specs=[spec(0, False), spec(pairs, True), spec(pairs, False),
                  spec(2 * pairs, True), spec(2 * pairs, False),
                  pl.BlockSpec((2, tq, 2 * tq), lambda b, hp, j: (hp, 0, 0))],
        out_specs=pl.BlockSpec((tq, 2 * HEAD_DIM), lambda b, hp, j: (b * nt + j, hp)),
        out_shape=jax.ShapeDtypeStruct((m, width), BF16),
        compiler_params=_cparams(("parallel", "parallel", "parallel")),
        name="chunk_attention",
    )(z_att, z_att, z_att, z_att, z_att, bias)


def _out_proj_kernel(x_ref, yr_ref, ya_ref, w1_ref, w2_ref, o_ref):
    o_ref[...] = x_ref[...] + _dot(yr_ref[...], w1_ref[...]) + _dot(ya_ref[...], w2_ref[...])


def _out_proj(x, y_rwkv, y_att, w_out, tm):
    m, d = x.shape
    wr = y_rwkv.shape[1]
    wa = y_att.shape[1]
    w16 = w_out.astype(BF16)
    return pl.pallas_call(
        _out_proj_kernel,
        grid=(m // tm,),
        in_specs=[pl.BlockSpec((tm, d), lambda i: (i, 0)),
                  pl.BlockSpec((tm, wr), lambda i: (i, 0)),
                  pl.BlockSpec((tm, wa), lambda i: (i, 0)),
                  pl.BlockSpec((wr, d), lambda i: (0, 0)),
                  pl.BlockSpec((wa, d), lambda i: (0, 0))],
        out_specs=pl.BlockSpec((tm, d), lambda i: (i, 0)),
        out_shape=jax.ShapeDtypeStruct((m, d), F32),
        compiler_params=_cparams(("parallel",)),
        name="out_proj",
    )(x, y_rwkv, y_att, w16[:wr], w16[wr:])


def _peer_query_kernel(h_ref, g_ref, w_ref, q_ref, xn_ref):
    xn = _rms(h_ref[...], g_ref[...]).astype(BF16)
    xn_ref[...] = xn
    q_ref[...] = _dot(xn, w_ref[...]).astype(q_ref.dtype)


def _peer_query(h, g, wq, tm):
    m, d = h.shape
    n = wq.shape[1]
    return pl.pallas_call(
        _peer_query_kernel,
        grid=(m // tm,),
        in_specs=[pl.BlockSpec((tm, d), lambda i: (i, 0)),
                  pl.BlockSpec((1, d), lambda i: (0, 0)),
                  pl.BlockSpec((d, n), lambda i: (0, 0))],
        out_specs=[pl.BlockSpec((tm, n), lambda i: (i, 0)),
                   pl.BlockSpec((tm, d), lambda i: (i, 0))],
        out_shape=[jax.ShapeDtypeStruct((m, n), BF16), jax.ShapeDtypeStruct((m, d), BF16)],
        compiler_params=_cparams(("parallel",)),
        name="peer_query",
    )(h, g.reshape(1, d), wq.astype(BF16))


def _top_values(s, count):
    rows = lax.broadcasted_iota(jnp.int32, (count, s.shape[1]), 0)
    tops = jnp.zeros((count, s.shape[1]), F32)
    cur = s
    for i in range(count):
        mx = jnp.max(cur, axis=0, keepdims=True)
        tops = jnp.where(rows == i, mx, tops)
        cur = jnp.where(cur == mx, -jnp.inf, cur)
    return tops


def _peer_score_kernel(q_ref, keys_ref, s0_ref, e0_ref, s1_ref, e1_ref, thr_ref):
    for h in range(PEER_HEADS):
        st = []
        for half in range(2):
            c0 = (2 * h + half) * PEER_HALF
            st.append(_dot_nt(keys_ref[2 * h + half], q_ref[:, c0:c0 + PEER_HALF]))
        top0 = _top_values(st[0], PEER_TOPK)
        top1 = _top_values(st[1], PEER_TOPK)
        cands = [top0[i:i + 1, :] + top1 for i in range(PEER_TOPK)]
        cur = list(cands)
        thr = None
        for _ in range(PEER_TOPK):
            mx = cur[0]
            for c in cur[1:]:
                mx = jnp.maximum(mx, c)
            thr = jnp.max(mx, axis=0, keepdims=True)
            cur = [jnp.where(c == thr, -jnp.inf, c) for c in cur]
        best = top0[0:1, :] + top1[0:1, :]
        den = jnp.zeros_like(thr)
        for c in cands:
            den = den + jnp.sum(jnp.where(c >= thr, jnp.exp(c - best), 0.0), axis=0, keepdims=True)
        s0_ref[h] = st[0]
        s1_ref[h] = st[1]
        e0_ref[h] = jnp.exp(st[0] - top0[0:1, :]) / den
        e1_ref[h] = jnp.exp(st[1] - top1[0:1, :])
        thr_ref[h:h + 1, :] = thr


def _peer_scores(q, sub_keys, tb):
    m = q.shape[0]
    keys = sub_keys.reshape(2 * PEER_HEADS, N_KEYS, PEER_HALF).astype(BF16)
    slab = jax.ShapeDtypeStruct((PEER_HEADS, N_KEYS, m), F32)
    slab_spec = pl.BlockSpec((PEER_HEADS, N_KEYS, tb), lambda i: (0, 0, i))
    return pl.pallas_call(
        _peer_score_kernel,
        grid=(m // tb,),
        in_specs=[pl.BlockSpec((tb, q.shape[1]), lambda i: (i, 0)),
                  pl.BlockSpec(keys.shape, lambda i: (0, 0, 0))],
        out_specs=[slab_spec, slab_spec, slab_spec, slab_spec,
                   pl.BlockSpec((PEER_HEADS, tb), lambda i: (0, i))],
        out_shape=[slab, slab, slab, slab, jax.ShapeDtypeStruct((PEER_HEADS, m), F32)],
        compiler_params=_cparams(("parallel",)),
        name="peer_scores",
    )(q, keys)


def _gelu_tanh(x):
    return 0.5 * x * (1.0 + jnp.tanh(0.7978845608028654 * (x + 0.044715 * x * x * x)))


def _peer_dense_kernel(xn_ref, u_ref, v_ref, s0_ref, e0_ref, s1_ref, e1_ref, thr_ref, h_ref,
                       o_ref, *, rows_per_step):
    e = pl.program_id(1)

    @pl.when(e == 0)
    def _():
        o_ref[...] = h_ref[...]

    act = _gelu_tanh(_dot_nt(u_ref[...], xn_ref[...]))
    gates = []
    for r in range(rows_per_step):
        i0 = e * rows_per_step + r
        gate = None
        for h in range(PEER_HEADS):
            tot = s1_ref[h] + s0_ref[h, pl.ds(i0, 1), :]
            term = jnp.where(tot >= thr_ref[h:h + 1, :], e1_ref[h] * e0_ref[h, pl.ds(i0, 1), :], 0.0)
            gate = term if gate is None else gate + term
        gates.append(gate)
    gate = gates[0] if rows_per_step == 1 else jnp.concatenate(gates, axis=0)
    o_ref[...] += _dot_tn((gate * act).astype(BF16), v_ref[...])


def _peer_dense(xn, u16, v16, s0, e0, s1, e1, thr, h, tb, eb):
    m, d = xn.shape
    n_exp = u16.shape[0]
    slab_spec = pl.BlockSpec((PEER_HEADS, N_KEYS, tb), lambda i, e: (0, 0, i))
    return pl.pallas_call(
        functools.partial(_peer_dense_kernel, rows_per_step=eb // N_KEYS),
        grid=(m // tb, n_exp // eb),
        in_specs=[pl.BlockSpec((tb, d), lambda i, e: (i, 0)),
                  pl.BlockSpec((eb, d), lambda i, e: (e, 0)),
                  pl.BlockSpec((eb, d), lambda i, e: (e, 0)),
                  slab_spec, slab_spec, slab_spec, slab_spec,
                  pl.BlockSpec((PEER_HEADS, tb), lambda i, e: (0, i)),
                  pl.BlockSpec((tb, d), lambda i, e: (i, 0))],
        out_specs=pl.BlockSpec((tb, d), lambda i, e: (i, 0)),
        out_shape=jax.ShapeDtypeStruct((m, d), F32),
        compiler_params=_cparams(("parallel", "arbitrary")),
        name="peer_dense",
    )(xn, u16, v16, s0, e0, s1, e1, thr, h)


def _ple_kernel(h_ref, g_ref, wg_ref, p_ref, wu_ref, fg_ref, o_ref):
    h = h_ref[...]
    xn = _rms(h, g_ref[...]).astype(BF16)
    gate = _sigmoid(_dot(xn, wg_ref[...]))
    up = _dot(p_ref[...].astype(BF16), wu_ref[...])
    o_ref[...] = _rms(h + gate * up, fg_ref[...])


def _ple_final(h, g, w_gate, p, w_up, final_g, tm):
    m, d = h.shape
    pd = p.shape[1]
    return pl.pallas_call(
        _ple_kernel,
        grid=(m // tm,),
        in_specs=[pl.BlockSpec((tm, d), lambda i: (i, 0)),
                  pl.BlockSpec((1, d), lambda i: (0, 0)),
                  pl.BlockSpec((d, d), lambda i: (0, 0)),
                  pl.BlockSpec((tm, pd), lambda i: (i, 0)),
                  pl.BlockSpec((pd, d), lambda i: (0, 0)),
                  pl.BlockSpec((1, d), lambda i: (0, 0))],
        out_specs=pl.BlockSpec((tm, d), lambda i: (i, 0)),
        out_shape=jax.ShapeDtypeStruct((m, d), F32),
        compiler_params=_cparams(("parallel",)),
        name="ple_final",
    )(h, g.reshape(1, d), w_gate.astype(BF16), p, w_up.astype(BF16), final_g.reshape(1, d))


def _pick(n, prefs):
    for t in prefs:
        if n % t == 0:
            return t
    return n


def kernel(x, p, mix_norm_g, w_in, rwkv_mu, decay_up, decay_w0, iclr_up, iclr_a0, gate_up, k_k, k_a, r_k, lnx_g, lnx_b, rel_bias, w_out, ffn_norm_g, peer_wq, peer_sub_keys, peer_u, peer_v, ple_norm_g, ple_gate_w, ple_up, final_norm_g):
    batch, seq, d = x.shape
    m = batch * seq
    assert w_in.shape[0] == 1, "the final norm is fused into the (single) layer's last stage"
    rwkv_w = decay_up.shape[2]
    rwkv_cols = rwkv_mu.shape[1]
    h = x.reshape(m, d)
    for i in range(1):
        w16 = w_in[i].astype(BF16)
        tm = _pick(m, (512, 256, 128))
        z_rwkv = _norm_matmul(h, mix_norm_g[i], w16[:, :rwkv_cols], F32, tm,
                              _pick(rwkv_cols, (1664, 1024, 512, 256, 128)))
        z_att = _norm_matmul(h, mix_norm_g[i], w16[:, rwkv_cols:], BF16, tm,
                             _pick(w16.shape[1] - rwkv_cols, (1536, 1024, 512, 256, 128)))
        y_rwkv = _rwkv(z_rwkv, batch, seq, rwkv_mu[i], decay_up[i], decay_w0[i], iclr_up[i],
                       iclr_a0[i], gate_up[i], k_k[i], k_a[i], r_k[i].reshape(rwkv_w),
                       lnx_g[i], lnx_b[i], _pick(seq, (256, 128, 64)))
        y_att = _attention(z_att, batch, seq, rel_bias[i], _pick(seq, (512,)))
        h1 = _out_proj(h, y_rwkv, y_att, w_out[i], _pick(m, (256, 128)))
        q, xn = _peer_query(h1, ffn_norm_g[i], peer_wq[i], _pick(m, (256, 128)))
        s0, e0, s1, e1, thr = _peer_scores(q, peer_sub_keys[i], _pick(m, (256, 128)))
        h2 = _peer_dense(xn, peer_u[i].astype(BF16), peer_v[i].astype(BF16), s0, e0, s1, e1, thr,
                         h1, _pick(m, (512, 256, 128)), 2 * N_KEYS)
        h = _ple_final(h2, ple_norm_g[i], ple_gate_w[i], p[i].reshape(m, -1), ple_up[i],
                       final_norm_g, _pick(m, (256, 128)))
    return h.reshape(batch, seq, d)
```

```python
import functools

import jax
import jax.numpy as jnp
from jax import lax
from jax.experimental import pallas as pl
from jax.experimental.pallas import tpu as pltpu

F32 = jnp.float32
BF16 = jnp.bfloat16

HEAD_DIM = 64
CHUNK = 64
LEFT_CHUNKS = 8
REL_CLIP = 128
DECAY_LORA = 64
AAA_LORA = 64
GATE_LORA = 128
PEER_HEADS = 8
N_KEYS = 128
PEER_HALF = 128
PEER_TOPK = 16
NORM_EPS = 1e-6
LNX_EPS = 64e-5

VMEM_LIMIT_BYTES = 56 * 1024 * 1024
GROUP_HEADS = 4
GROUP_W = GROUP_HEADS * HEAD_DIM


def _cparams(sem, flags=None):
    return pltpu.CompilerParams(dimension_semantics=sem, vmem_limit_bytes=VMEM_LIMIT_BYTES,
                                flags=flags)


def _dot(a, b):
    return jnp.dot(a, b, preferred_element_type=F32)


def _dot_nt(a, b):
    return lax.dot_general(a, b, (((1,), (1,)), ((), ())), preferred_element_type=F32)


def _dot_tn(a, b):
    return lax.dot_general(a, b, (((0,), (0,)), ((), ())), preferred_element_type=F32)


def _rms(x, g):
    return x * lax.rsqrt(jnp.mean(x * x, axis=-1, keepdims=True) + NORM_EPS) * g


def _sigmoid(x):
    return 1.0 / (1.0 + jnp.exp(-x))


def _norm_matmul_kernel(x_ref, g_ref, w_ref, o_ref, xn_ref):
    @pl.when(pl.program_id(1) == 0)
    def _():
        xn_ref[...] = _rms(x_ref[...], g_ref[...]).astype(BF16)

    o_ref[...] = _dot(xn_ref[...], w_ref[...]).astype(o_ref.dtype)


def _norm_matmul(x, g, w, out_dtype, tm, tn):
    m, d = x.shape
    n = w.shape[1]
    return pl.pallas_call(
        _norm_matmul_kernel,
        grid=(m // tm, n // tn),
        in_specs=[
            pl.BlockSpec((tm, d), lambda i, j: (i, 0)),
            pl.BlockSpec((1, d), lambda i, j: (0, 0)),
            pl.BlockSpec((d, tn), lambda i, j: (0, j)),
        ],
        out_specs=pl.BlockSpec((tm, tn), lambda i, j: (i, j)),
        out_shape=jax.ShapeDtypeStruct((m, n), out_dtype),
        scratch_shapes=[pltpu.VMEM((tm, d), BF16)],
        compiler_params=_cparams(("parallel", "arbitrary")),
        name="norm_matmul",
    )(x, g.reshape(1, d), w)


def _split_dot(x, w_bf16):
    hi = x.astype(BF16)
    lo = (x - hi.astype(F32)).astype(BF16)
    return _dot(hi, w_bf16) + _dot(lo, w_bf16)


def _rwkv_kernel(zr_ref, zk_ref, zv_ref, zl_ref, mur_ref, muk_ref, muv_ref, mul_ref,
                 dup_ref, dw0_ref, iup_ref, ia0_ref, gup_ref, kk_ref, ka_ref, rk_ref,
                 lg_ref, lb_ref, y_ref, state_ref, last_ref, *, tm):
    j = pl.program_id(2)
    n_chunks = tm // CHUNK

    @pl.when(j == 0)
    def _():
        state_ref[...] = jnp.zeros_like(state_ref)
        last_ref[...] = jnp.zeros_like(last_ref)

    row = lax.broadcasted_iota(jnp.int32, (tm, GROUP_W), 0)

    def shift_mix(ref, mu_ref, slot):
        cur = ref[...]
        prev = pltpu.roll(cur, 1, axis=0)
        prev = jnp.where(row == 0, last_ref[slot:slot + 1, :], prev)
        last_ref[slot:slot + 1, :] = cur[tm - 1:tm, :]
        return cur + (prev - cur) * mu_ref[...]

    r = shift_mix(zr_ref, mur_ref, 0)
    k = shift_mix(zk_ref, muk_ref, 1)
    v = shift_mix(zv_ref, muv_ref, 2)
    lo = shift_mix(zl_ref, mul_ref, 3)
    wd = lo[:, :DECAY_LORA]
    ad = lo[:, DECAY_LORA:DECAY_LORA + AAA_LORA]
    gd = lo[:, DECAY_LORA + AAA_LORA:]

    dec = dw0_ref[...] + _dot(jnp.tanh(wd).astype(BF16), dup_ref[...])
    neg = -dec
    softplus = jnp.maximum(neg, 0.0) + jnp.log(1.0 + jnp.exp(-jnp.abs(neg)))
    logw = -jnp.exp(-softplus - 0.5)
    a = _sigmoid(ia0_ref[...] + _dot(ad.astype(BF16), iup_ref[...]))
    g = _dot(_sigmoid(gd).astype(BF16), gup_ref[...])

    hrow = lax.broadcasted_iota(jnp.int32, (GROUP_W, GROUP_W), 0)
    hcol = lax.broadcasted_iota(jnp.int32, (GROUP_W, GROUP_W), 1)
    same = (hrow // HEAD_DIM) == (hcol // HEAD_DIM)
    ones_bd = jnp.where(same, 1.0, 0.0).astype(BF16)

    kkr = k * kk_ref[...]
    kk_norm = jnp.sqrt(_split_dot(kkr * kkr, ones_bd))
    kk = kkr / jnp.maximum(kk_norm, 1e-12)
    k2 = k * (1.0 + (a - 1.0) * ka_ref[...])
    bonus = _split_dot(r * k2 * rk_ref[...], ones_bd) * v
    avec = -kk
    bvec = kk * a

    t_row = hrow % CHUNK
    t_col = hcol % CHUNK
    strict = same & (t_row > t_col)
    incl = same & (t_row >= t_col)
    eye = hrow == hcol
    tri = jnp.where(lax.broadcasted_iota(jnp.int32, (CHUNK, CHUNK), 0)
                    >= lax.broadcasted_iota(jnp.int32, (CHUNK, CHUNK), 1), 1.0, 0.0).astype(BF16)

    def tile4(x):
        return jnp.concatenate([x] * GROUP_HEADS, axis=0)

    def bd16(x):
        return tile4(x.astype(BF16)) * ones_bd

    strict16 = jnp.where(strict, 1.0, 0.0).astype(BF16)
    incl16 = jnp.where(incl, 1.0, 0.0).astype(BF16)
    strict32 = jnp.where(strict, 1.0, 0.0)
    eye32 = jnp.where(eye, 1.0, 0.0)
    chunks = range(n_chunks)
    sls = [slice(c * CHUNK, (c + 1) * CHUNK) for c in chunks]

    cums = []
    for c in chunks:
        lw = logw[sls[c]]
        lw_hi = lw.astype(BF16)
        rem = lw - lw_hi.astype(F32)
        lw_mid = rem.astype(BF16)
        lw_lo = (rem - lw_mid.astype(F32)).astype(BF16)
        cums.append(_dot(tri, lw_hi) + _dot(tri, lw_mid) + _dot(tri, lw_lo))
    bd_a, bd_r, bd_v, bd_be, bd_ke, rhs, p_end, r_hat = [], [], [], [], [], [], [], []
    for c in chunks:
        sl, cum = sls[c], cums[c]
        lw = logw[sl]
        e_cum = jnp.exp(cum)
        e_inv = jnp.exp(-cum)
        e_end = jnp.exp(cum[CHUNK - 1:CHUNK, :] - cum)
        rh = r[sl] * e_cum
        r_hat.append(rh)
        bd_a.append(bd16(avec[sl] * jnp.exp(cum - lw)))
        bd_r.append(bd16(rh))
        bd_v.append(bd16(v[sl]))
        bd_be.append(bd16(bvec[sl] * e_end))
        bd_ke.append(bd16(k2[sl] * e_end))
        rhs.append(jnp.concatenate([tile4((bvec[sl] * e_inv).astype(BF16)),
                                    tile4((k2[sl] * e_inv).astype(BF16))], axis=0))
        p_end.append(e_cum[CHUNK - 1:CHUNK, :])
    lm = [_dot_nt(jnp.concatenate([bd_a[c], bd_r[c]], axis=0), rhs[c]) for c in chunks]
    l_pow = [lm[c][:GROUP_W, :GROUP_W] * strict32 for c in chunks]
    l_ak = [lm[c][:GROUP_W, GROUP_W:].astype(BF16) * strict16 for c in chunks]
    m_rb = [lm[c][GROUP_W:, :GROUP_W].astype(BF16) * incl16 for c in chunks]
    m_rk = [lm[c][GROUP_W:, GROUP_W:].astype(BF16) * incl16 for c in chunks]
    t_inv = [eye32 + l_pow[c] for c in chunks]
    for _ in range(5):
        lp16 = [l_pow[c].astype(BF16) for c in chunks]
        l_pow = [_dot(lp16[c], lp16[c]) for c in chunks]
        t_inv = [t_inv[c] + _dot(t_inv[c].astype(BF16), l_pow[c].astype(BF16)) for c in chunks]
    lakv = [_dot(l_ak[c], bd_v[c]).astype(BF16) for c in chunks]
    tw16 = [_dot(t_inv[c].astype(BF16), jnp.concatenate([bd_a[c], lakv[c]], axis=1)).astype(BF16)
            for c in chunks]
    x1 = [_dot_tn(bd_be[c], tw16[c]) for c in chunks]
    x2 = [_dot_tn(bd_ke[c], bd_v[c]) for c in chunks]
    mt = [_dot(m_rb[c], tw16[c]) for c in chunks]
    mv = [_dot(m_rk[c], bd_v[c]) for c in chunks]
    a_c = [(eye32 * p_end[c] + x1[c][:, :GROUP_W]).astype(BF16) for c in chunks]
    c_c = [x1[c][:, GROUP_W:] + x2[c] for c in chunks]
    q_c = [(jnp.where(same, tile4(r_hat[c]), 0.0) + mt[c][:, :GROUP_W]).astype(BF16)
           for c in chunks]
    y_loc = [mt[c][:, GROUP_W:] + mv[c] for c in chunks]
    ys = []
    state = state_ref[...]
    for c in chunks:
        s16 = state.astype(BF16)
        y_bd = _dot(q_c[c], s16) + y_loc[c]
        state = _dot(a_c[c], s16) + c_c[c]
        y_w = y_bd[0:CHUNK]
        for h in range(1, GROUP_HEADS):
            y_w = y_w + y_bd[h * CHUNK:(h + 1) * CHUNK]
        ys.append(y_w)
    state_ref[...] = state


    y = jnp.concatenate(ys, axis=0)
    mean = _split_dot(y, ones_bd) * (1.0 / HEAD_DIM)
    yc = y - mean
    var = _split_dot(yc * yc, ones_bd) * (1.0 / HEAD_DIM)
    yn = yc * lax.rsqrt(var + LNX_EPS) * lg_ref[...] + lb_ref[...]
    y_ref[...] = ((yn + bonus) * g).astype(y_ref.dtype)


def _rwkv(z, batch, seq, mu, decay_up, decay_w0, iclr_up, iclr_a0, gate_up, k_k, k_a, r_k,
          lnx_g, lnx_b, tm):
    m = z.shape[0]
    width = decay_up.shape[1]
    groups = width // GROUP_W
    nt = seq // tm
    lora_blk = 3 * groups

    def zspec(off):
        return pl.BlockSpec((tm, GROUP_W), lambda b, g, j: (b * nt + j, off(g)))

    def muspec(off):
        return pl.BlockSpec((1, GROUP_W), lambda b, g, j: (0, off(g)))

    def pspec(rows):
        return pl.BlockSpec((rows, GROUP_W), lambda b, g, j: (0, g))

    offs = [lambda g: g, lambda g: groups + g, lambda g: 2 * groups + g, lambda g: lora_blk]
    vec = lambda t: t.reshape(1, width)
    mu2 = mu.reshape(1, -1)
    return pl.pallas_call(
        functools.partial(_rwkv_kernel, tm=tm),
        grid=(batch, groups, nt),
        in_specs=[zspec(o) for o in offs] + [muspec(o) for o in offs] + [
            pspec(DECAY_LORA), pspec(1), pspec(AAA_LORA), pspec(1), pspec(GATE_LORA),
            pspec(1), pspec(1), pspec(1), pspec(1), pspec(1)],
        out_specs=pl.BlockSpec((tm, GROUP_W), lambda b, g, j: (b * nt + j, g)),
        out_shape=jax.ShapeDtypeStruct((m, width), BF16),
        scratch_shapes=[pltpu.VMEM((GROUP_W, GROUP_W), F32), pltpu.VMEM((8, GROUP_W), F32)],
        compiler_params=_cparams(("parallel", "parallel", "arbitrary")),
        name="rwkv7",
    )(z, z, z, z, mu2, mu2, mu2, mu2,
      decay_up.astype(BF16), vec(decay_w0), iclr_up.astype(BF16), vec(iclr_a0),
      gate_up.astype(BF16), vec(k_k), vec(k_a), vec(r_k), vec(lnx_g), vec(lnx_b))


def _attn_kernel(q_ref, kp_ref, kc_ref, vp_ref, vc_ref, bias_ref, o_ref, *, tq):
    j = pl.program_id(2)
    q = q_ref[...]
    kcat = jnp.concatenate([kp_ref[...], kc_ref[...]], axis=0)
    vcat = jnp.concatenate([vp_ref[...], vc_ref[...]], axis=0)
    lane = lax.broadcasted_iota(jnp.int32, (tq, 2 * HEAD_DIM), 1)
    col = lax.broadcasted_iota(jnp.int32, (tq, 2 * tq), 1)
    in_seq = (col >= tq) | (j > 0)
    outs = []
    for hh in range(2):
        mine = (lane // HEAD_DIM) == hh
        qm = jnp.where(mine, q, jnp.zeros_like(q)) * (HEAD_DIM ** -0.5)
        s = _dot_nt(qm.astype(BF16), kcat) + bias_ref[hh]
        s = jnp.where(in_seq, s, -jnp.inf)
        mx = jnp.max(s, axis=-1, keepdims=True)
        p = jnp.exp(s - mx)
        den = jnp.sum(p, axis=-1, keepdims=True)
        outs.append(_dot(p.astype(BF16), vcat) / den)
    o_ref[...] = jnp.where((lane // HEAD_DIM) == 0, outs[0], outs[1]).astype(o_ref.dtype)


def _attention(z_att, batch, seq, rel_bias, tq):
    m, w3 = z_att.shape
    width = w3 // 3
    heads = width // HEAD_DIM
    pairs = heads // 2
    nt = seq // tq
    assert tq == LEFT_CHUNKS * CHUNK
    band_w = (LEFT_CHUNKS + 1) * CHUNK
    qi = jnp.arange(CHUNK)[:, None] + LEFT_CHUNKS * CHUNK
    kj = jnp.arange(band_w)[None, :]
    rel = jnp.clip(qi - kj, -REL_CLIP, REL_CLIP) + REL_CLIP
    band_bias = rel_bias[:, rel].astype(F32)
    bias = jnp.concatenate(
        [jnp.pad(band_bias, ((0, 0), (0, 0), (c * CHUNK, 2 * tq - band_w - c * CHUNK)),
                 constant_values=-jnp.inf) for c in range(tq // CHUNK)], axis=1)

    def spec(col_off, prev):
        if prev:
            return pl.BlockSpec((tq, 2 * HEAD_DIM),
                                lambda b, hp, j: (b * nt + jnp.maximum(j - 1, 0), col_off + hp))
        return pl.BlockSpec((tq, 2 * HEAD_DIM), lambda b, hp, j: (b * nt + j, col_off + hp))

    return pl.pallas_call(
        functools.partial(_attn_kernel, tq=tq),
        grid=(batch, pairs, nt),
        in_specs=[spec(0, False), spec(pairs, True), spec(pairs, False),
                  spec(2 * pairs, True), spec(2 * pairs, False),
                  pl.BlockSpec((2, tq, 2 * tq), lambda b, hp, j: (hp, 0, 0))],
        out_specs=pl.BlockSpec((tq, 2 * HEAD_DIM), lambda b, hp, j: (b * nt + j, hp)),
        out_shape=jax.ShapeDtypeStruct((m, width), BF16),
        compiler_params=_cparams(("parallel", "parallel", "parallel")),
        name="chunk_attention",
    )(z_att, z_att, z_att, z_att, z_att, bias)


def _out_proj_kernel(x_ref, yr_ref, ya_ref, w1_ref, w2_ref, o_ref):
    o_ref[...] = x_ref[...] + _dot(yr_ref[...], w1_ref[...]) + _dot(ya_ref[...], w2_ref[...])


def _out_proj(x, y_rwkv, y_att, w_out, tm):
    m, d = x.shape
    wr = y_rwkv.shape[1]
    wa = y_att.shape[1]
    w16 = w_out.astype(BF16)
    return pl.pallas_call(
        _out_proj_kernel,
        grid=(m // tm,),
        in_specs=[pl.BlockSpec((tm, d), lambda i: (i, 0)),
                  pl.BlockSpec((tm, wr), lambda i: (i, 0)),
                  pl.BlockSpec((tm, wa), lambda i: (i, 0)),
                  pl.BlockSpec((wr, d), lambda i: (0, 0)),
                  pl.BlockSpec((wa, d), lambda i: (0, 0))],
        out_specs=pl.BlockSpec((tm, d), lambda i: (i, 0)),
        out_shape=jax.ShapeDtypeStruct((m, d), F32),
        compiler_params=_cparams(("parallel",)),
        name="out_proj",
    )(x, y_rwkv, y_att, w16[:wr], w16[wr:])


def _peer_query_kernel(h_ref, g_ref, w_ref, q_ref, xnt_ref):
    xn = _rms(h_ref[...], g_ref[...])
    xnt_ref[...] = xn.T.astype(BF16)
    q_ref[...] = _dot(xn.astype(BF16), w_ref[...]).astype(q_ref.dtype)


def _peer_query(h, g, wq, tm):
    m, d = h.shape
    n = wq.shape[1]
    return pl.pallas_call(
        _peer_query_kernel,
        grid=(m // tm,),
        in_specs=[pl.BlockSpec((tm, d), lambda i: (i, 0)),
                  pl.BlockSpec((1, d), lambda i: (0, 0)),
                  pl.BlockSpec((d, n), lambda i: (0, 0))],
        out_specs=[pl.BlockSpec((tm, n), lambda i: (i, 0)),
                   pl.BlockSpec((d, tm), lambda i: (0, i))],
        out_shape=[jax.ShapeDtypeStruct((m, n), BF16), jax.ShapeDtypeStruct((d, m), BF16)],
        compiler_params=_cparams(("parallel",)),
        name="peer_query",
    )(h, g.reshape(1, d), wq.astype(BF16))


def _top_values(s, count):
    rows = lax.broadcasted_iota(jnp.int32, (count, s.shape[1]), 0)
    tops = jnp.zeros((count, s.shape[1]), F32)
    cur = s
    for i in range(count):
        mx = jnp.max(cur, axis=0, keepdims=True)
        tops = jnp.where(rows == i, mx, tops)
        cur = jnp.where(cur == mx, -jnp.inf, cur)
    return tops


def _peer_score_kernel(q_ref, keys_ref, s0_ref, e0_ref, s1_ref, e1_ref, thr_ref):
    for h in range(PEER_HEADS):
        st = []
        for half in range(2):
            c0 = (2 * h + half) * PEER_HALF
            st.append(_dot_nt(keys_ref[2 * h + half], q_ref[:, c0:c0 + PEER_HALF]))
        top0 = _top_values(st[0], PEER_TOPK)
        top1 = _top_values(st[1], PEER_TOPK)
        cands = [top0[i:i + 1, :] + top1 for i in range(PEER_TOPK)]
        cur = list(cands)
        thr = None
        for _ in range(PEER_TOPK):
            mx = cur[0]
            for c in cur[1:]:
                mx = jnp.maximum(mx, c)
            thr = jnp.max(mx, axis=0, keepdims=True)
            cur = [jnp.where(c == thr, -jnp.inf, c) for c in cur]
        best = top0[0:1, :] + top1[0:1, :]
        den = jnp.zeros_like(thr)
        for c in cands:
            den = den + jnp.sum(jnp.where(c >= thr, jnp.exp(c - best), 0.0), axis=0, keepdims=True)
        s0_ref[h] = st[0]
        s1_ref[h] = st[1]
        e0_ref[h] = jnp.exp(st[0] - top0[0:1, :]) / den
        e1_ref[h] = jnp.exp(st[1] - top1[0:1, :])
        thr_ref[h:h + 1, :] = thr


def _peer_scores(q, sub_keys, tb):
    m = q.shape[0]
    keys = sub_keys.reshape(2 * PEER_HEADS, N_KEYS, PEER_HALF).astype(BF16)
    slab = jax.ShapeDtypeStruct((PEER_HEADS, N_KEYS, m), F32)
    slab_spec = pl.BlockSpec((PEER_HEADS, N_KEYS, tb), lambda i: (0, 0, i))
    return pl.pallas_call(
        _peer_score_kernel,
        grid=(m // tb,),
        in_specs=[pl.BlockSpec((tb, q.shape[1]), lambda i: (i, 0)),
                  pl.BlockSpec(keys.shape, lambda i: (0, 0, 0))],
        out_specs=[slab_spec, slab_spec, slab_spec, slab_spec,
                   pl.BlockSpec((PEER_HEADS, tb), lambda i: (0, i))],
        out_shape=[slab, slab, slab, slab, jax.ShapeDtypeStruct((PEER_HEADS, m), F32)],
        compiler_params=_cparams(("parallel",)),
        name="peer_scores",
    )(q, keys)


def _gelu_tanh(x):
    return 0.5 * x * (1.0 + jnp.tanh(0.7978845608028654 * (x + 0.044715 * x * x * x)))


LANES = 128
SUBLANES = 8


def _peer_dense_kernel(xnt_ref, u_ref, vt_ref, s0_ref, e0_ref, s1_ref, e1_ref, thr_ref,
                       ot_ref, act_ref, w_ref, *, rows_per_step, n_blocks):
    e = pl.program_id(1)
    tb = xnt_ref.shape[1]

    @pl.when(e == 0)
    def _():
        ot_ref[...] = jnp.zeros_like(ot_ref)
        act_ref[...] = jnp.zeros_like(act_ref)
        w_ref[...] = jnp.zeros_like(w_ref)

    cur = e % 2
    prv = 1 - cur
    act_ref[cur] = _dot(u_ref[...], xnt_ref[...])
    ot_ref[...] += _dot(vt_ref[...], w_ref[cur])

    blk = jnp.clip(e - 1, 0, n_blocks - 1)
    for r in range(rows_per_step):
        i0 = blk * rows_per_step + r
        s0_rows = [s0_ref[h, pl.ds(i0, 1), :] for h in range(PEER_HEADS)]
        e0_rows = [e0_ref[h, pl.ds(i0, 1), :] for h in range(PEER_HEADS)]
        for tc in range(tb // LANES):
            lanes = slice(tc * LANES, (tc + 1) * LANES)
            thr_b = [jnp.broadcast_to(thr_ref[h:h + 1, lanes], (SUBLANES, LANES))
                     for h in range(PEER_HEADS)]
            s0_b = [jnp.broadcast_to(s0_rows[h][:, lanes], (SUBLANES, LANES))
                    for h in range(PEER_HEADS)]
            e0_b = [jnp.broadcast_to(e0_rows[h][:, lanes], (SUBLANES, LANES))
                    for h in range(PEER_HEADS)]
            for ib in range(N_KEYS // (2 * SUBLANES)):
                halves = []
                for half in range(2):
                    k0 = (2 * ib + half) * SUBLANES
                    krows = slice(k0, k0 + SUBLANES)
                    gate = None
                    for h in range(PEER_HEADS):
                        tot = s1_ref[h, krows, lanes] + s0_b[h]
                        term = jnp.where(tot >= thr_b[h], e1_ref[h, krows, lanes] * e0_b[h], 0.0)
                        gate = term if gate is None else gate + term
                    halves.append(gate)
                rows = slice(r * N_KEYS + 2 * ib * SUBLANES, r * N_KEYS + 2 * (ib + 1) * SUBLANES)
                gate16 = jnp.concatenate(halves, axis=0)
                w_ref[prv, rows, lanes] = (gate16 * _gelu_tanh(act_ref[prv, rows, lanes])).astype(BF16)


def _peer_dense(xnt, u16, vt16, s0, e0, s1, e1, thr, tb, eb):
    d, m = xnt.shape
    n_blocks = u16.shape[0] // eb
    slab_spec = pl.BlockSpec((PEER_HEADS, N_KEYS, tb), lambda i, e: (0, 0, i))
    return pl.pallas_call(
        functools.partial(_peer_dense_kernel, rows_per_step=eb // N_KEYS, n_blocks=n_blocks),
        grid=(m // tb, n_blocks + 2),
        in_specs=[pl.BlockSpec((d, tb), lambda i, e: (0, i)),
                  pl.BlockSpec((eb, d), lambda i, e: (jnp.minimum(e, n_blocks - 1), 0)),
                  pl.BlockSpec((d, eb), lambda i, e: (0, jnp.clip(e - 2, 0, n_blocks - 1))),
                  slab_spec, slab_spec, slab_spec, slab_spec,
                  pl.BlockSpec((PEER_HEADS, tb), lambda i, e: (0, i))],
        out_specs=pl.BlockSpec((d, tb), lambda i, e: (0, i)),
        out_shape=jax.ShapeDtypeStruct((d, m), F32),
        scratch_shapes=[pltpu.VMEM((2, eb, tb), F32), pltpu.VMEM((2, eb, tb), BF16)],
        compiler_params=_cparams(("parallel", "arbitrary")),
        name="peer_dense",
    )(xnt, u16, vt16, s0, e0, s1, e1, thr)


def _ple_kernel(h_ref, po_ref, g_ref, wg_ref, p_ref, wu_ref, fg_ref, o_ref):
    h = h_ref[...] + po_ref[...].T
    xn = _rms(h, g_ref[...]).astype(BF16)
    gate = _sigmoid(_dot(xn, wg_ref[...]))
    up = _dot(p_ref[...].astype(BF16), wu_ref[...])
    o_ref[...] = _rms(h + gate * up, fg_ref[...])


def _ple_final(h, peer_out, g, w_gate, p, w_up, final_g, tm):
    m, d = h.shape
    pd = p.shape[1]
    return pl.pallas_call(
        _ple_kernel,
        grid=(m // tm,),
        in_specs=[pl.BlockSpec((tm, d), lambda i: (i, 0)),
                  pl.BlockSpec((d, tm), lambda i: (0, i)),
                  pl.BlockSpec((1, d), lambda i: (0, 0)),
                  pl.BlockSpec((d, d), lambda i: (0, 0)),
                  pl.BlockSpec((tm, pd), lambda i: (i, 0)),
                  pl.BlockSpec((pd, d), lambda i: (0, 0)),
                  pl.BlockSpec((1, d), lambda i: (0, 0))],
        out_specs=pl.BlockSpec((tm, d), lambda i: (i, 0)),
        out_shape=jax.ShapeDtypeStruct((m, d), F32),
        compiler_params=_cparams(("parallel",)),
        name="ple_final",
    )(h, peer_out, g.reshape(1, d), w_gate.astype(BF16), p, w_up.astype(BF16),
      final_g.reshape(1, d))


def _pick(n, prefs):
    for t in prefs:
        if n % t == 0:
            return t
    return n


def kernel(x, p, mix_norm_g, w_in, rwkv_mu, decay_up, decay_w0, iclr_up, iclr_a0, gate_up, k_k, k_a, r_k, lnx_g, lnx_b, rel_bias, w_out, ffn_norm_g, peer_wq, peer_sub_keys, peer_u, peer_v, ple_norm_g, ple_gate_w, ple_up, final_norm_g):
    batch, seq, d = x.shape
    m = batch * seq
    assert w_in.shape[0] == 1, "the final norm is fused into the (single) layer's last stage"
    rwkv_w = decay_up.shape[2]
    rwkv_cols = rwkv_mu.shape[1]
    h = x.reshape(m, d)
    for i in range(1):
        w16 = w_in[i].astype(BF16)
        tm = _pick(m, (512, 256, 128))
        z_rwkv = _norm_matmul(h, mix_norm_g[i], w16[:, :rwkv_cols], F32, tm,
                              _pick(rwkv_cols, (1664, 1024, 512, 256, 128)))
        z_att = _norm_matmul(h, mix_norm_g[i], w16[:, rwkv_cols:], BF16, tm,
                             _pick(w16.shape[1] - rwkv_cols, (1536, 1024, 512, 256, 128)))
        y_rwkv = _rwkv(z_rwkv, batch, seq, rwkv_mu[i], decay_up[i], decay_w0[i], iclr_up[i],
                       iclr_a0[i], gate_up[i], k_k[i], k_a[i], r_k[i].reshape(rwkv_w),
                       lnx_g[i], lnx_b[i], _pick(seq, (256, 128, 64)))
        y_att = _attention(z_att, batch, seq, rel_bias[i], _pick(seq, (512,)))
        h1 = _out_proj(h, y_rwkv, y_att, w_out[i], _pick(m, (256, 128)))
        q, xnt = _peer_query(h1, ffn_norm_g[i], peer_wq[i], _pick(m, (256, 128)))
        s0, e0, s1, e1, thr = _peer_scores(q, peer_sub_keys[i], _pick(m, (256, 128)))
        peer_out_t = _peer_dense(xnt, peer_u[i].astype(BF16), peer_v[i].T.astype(BF16), s0, e0, s1,
                                 e1, thr, _pick(m, (512, 256, 128)), 4 * N_KEYS)
        h = _ple_final(h1, peer_out_t, ple_norm_g[i], ple_gate_w[i], p[i].reshape(m, -1), ple_up[i],
                       final_norm_g, _pick(m, (256, 128)))
    return h.reshape(batch, seq, d)
```

```python
import functools

import jax
import jax.numpy as jnp
from jax import lax
from jax.experimental import pallas as pl
from jax.experimental.pallas import tpu as pltpu

F32 = jnp.float32
BF16 = jnp.bfloat16

HEAD_DIM = 64
CHUNK = 64
LEFT_CHUNKS = 8
REL_CLIP = 128
DECAY_LORA = 64
AAA_LORA = 64
GATE_LORA = 128
PEER_HEADS = 8
N_KEYS = 128
PEER_HALF = 128
PEER_TOPK = 16
NORM_EPS = 1e-6
LNX_EPS = 64e-5

VMEM_LIMIT_BYTES = 56 * 1024 * 1024
GROUP_HEADS = 4
GROUP_W = GROUP_HEADS * HEAD_DIM


def _cparams(sem, flags=None):
    return pltpu.CompilerParams(dimension_semantics=sem, vmem_limit_bytes=VMEM_LIMIT_BYTES,
                                flags=flags)


def _dot(a, b):
    return jnp.dot(a, b, preferred_element_type=F32)


def _dot_nt(a, b):
    return lax.dot_general(a, b, (((1,), (1,)), ((), ())), preferred_element_type=F32)


def _dot_tn(a, b):
    return lax.dot_general(a, b, (((0,), (0,)), ((), ())), preferred_element_type=F32)


def _rms(x, g):
    return x * lax.rsqrt(jnp.mean(x * x, axis=-1, keepdims=True) + NORM_EPS) * g


def _sigmoid(x):
    return 1.0 / (1.0 + jnp.exp(-x))


def _norm_matmul_kernel(x_ref, g_ref, w_ref, o_ref, xn_ref):
    @pl.when(pl.program_id(1) == 0)
    def _():
        xn_ref[...] = _rms(x_ref[...], g_ref[...]).astype(BF16)

    o_ref[...] = _dot(xn_ref[...], w_ref[...]).astype(o_ref.dtype)


def _norm_matmul(x, g, w, out_dtype, tm, tn):
    m, d = x.shape
    n = w.shape[1]
    return pl.pallas_call(
        _norm_matmul_kernel,
        grid=(m // tm, n // tn),
        in_specs=[
            pl.BlockSpec((tm, d), lambda i, j: (i, 0)),
            pl.BlockSpec((1, d), lambda i, j: (0, 0)),
            pl.BlockSpec((d, tn), lambda i, j: (0, j)),
        ],
        out_specs=pl.BlockSpec((tm, tn), lambda i, j: (i, j)),
        out_shape=jax.ShapeDtypeStruct((m, n), out_dtype),
        scratch_shapes=[pltpu.VMEM((tm, d), BF16)],
        compiler_params=_cparams(("parallel", "arbitrary")),
        name="norm_matmul",
    )(x, g.reshape(1, d), w)


def _split_dot(x, w_bf16):
    hi = x.astype(BF16)
    lo = (x - hi.astype(F32)).astype(BF16)
    return _dot(hi, w_bf16) + _dot(lo, w_bf16)


def _rwkv_kernel(zr_ref, zk_ref, zv_ref, zl_ref, mur_ref, muk_ref, muv_ref, mul_ref,
                 dup_ref, dw0_ref, iup_ref, ia0_ref, gup_ref, kk_ref, ka_ref, rk_ref,
                 lg_ref, lb_ref, y_ref, state_ref, last_ref, *, tm):
    j = pl.program_id(2)
    n_chunks = tm // CHUNK

    @pl.when(j == 0)
    def _():
        state_ref[...] = jnp.zeros_like(state_ref)
        last_ref[...] = jnp.zeros_like(last_ref)

    row = lax.broadcasted_iota(jnp.int32, (tm, GROUP_W), 0)

    def shift_mix(ref, mu_ref, slot):
        cur = ref[...]
        prev = pltpu.roll(cur, 1, axis=0)
        prev = jnp.where(row == 0, last_ref[slot:slot + 1, :], prev)
        last_ref[slot:slot + 1, :] = cur[tm - 1:tm, :]
        return cur + (prev - cur) * mu_ref[...]

    r = shift_mix(zr_ref, mur_ref, 0)
    k = shift_mix(zk_ref, muk_ref, 1)
    v = shift_mix(zv_ref, muv_ref, 2)
    lo = shift_mix(zl_ref, mul_ref, 3)
    wd = lo[:, :DECAY_LORA]
    ad = lo[:, DECAY_LORA:DECAY_LORA + AAA_LORA]
    gd = lo[:, DECAY_LORA + AAA_LORA:]

    dec = dw0_ref[...] + _dot(jnp.tanh(wd).astype(BF16), dup_ref[...])
    neg = -dec
    softplus = jnp.maximum(neg, 0.0) + jnp.log(1.0 + jnp.exp(-jnp.abs(neg)))
    logw = -jnp.exp(-softplus - 0.5)
    a = _sigmoid(ia0_ref[...] + _dot(ad.astype(BF16), iup_ref[...]))
    g = _dot(_sigmoid(gd).astype(BF16), gup_ref[...])

    hrow = lax.broadcasted_iota(jnp.int32, (GROUP_W, GROUP_W), 0)
    hcol = lax.broadcasted_iota(jnp.int32, (GROUP_W, GROUP_W), 1)
    same = (hrow // HEAD_DIM) == (hcol // HEAD_DIM)
    ones_bd = jnp.where(same, 1.0, 0.0).astype(BF16)

    kkr = k * kk_ref[...]
    kk_norm = jnp.sqrt(_split_dot(kkr * kkr, ones_bd))
    kk = kkr / jnp.maximum(kk_norm, 1e-12)
    k2 = k * (1.0 + (a - 1.0) * ka_ref[...])
    bonus = _split_dot(r * k2 * rk_ref[...], ones_bd) * v
    avec = -kk
    bvec = kk * a

    t_row = hrow % CHUNK
    t_col = hcol % CHUNK
    strict = same & (t_row > t_col)
    incl = same & (t_row >= t_col)
    eye = hrow == hcol
    tri = jnp.where(lax.broadcasted_iota(jnp.int32, (CHUNK, CHUNK), 0)
                    >= lax.broadcasted_iota(jnp.int32, (CHUNK, CHUNK), 1), 1.0, 0.0).astype(BF16)

    def tile4(x):
        return jnp.concatenate([x] * GROUP_HEADS, axis=0)

    def bd16(x):
        return tile4(x.astype(BF16)) * ones_bd

    strict16 = jnp.where(strict, 1.0, 0.0).astype(BF16)
    incl16 = jnp.where(incl, 1.0, 0.0).astype(BF16)
    strict32 = jnp.where(strict, 1.0, 0.0)
    eye32 = jnp.where(eye, 1.0, 0.0)
    chunks = range(n_chunks)
    sls = [slice(c * CHUNK, (c + 1) * CHUNK) for c in chunks]

    cums = []
    for c in chunks:
        lw = logw[sls[c]]
        lw_hi = lw.astype(BF16)
        rem = lw - lw_hi.astype(F32)
        lw_mid = rem.astype(BF16)
        lw_lo = (rem - lw_mid.astype(F32)).astype(BF16)
        cums.append(_dot(tri, lw_hi) + _dot(tri, lw_mid) + _dot(tri, lw_lo))
    bd_a, bd_r, bd_v, bd_be, bd_ke, rhs, p_end, r_hat = [], [], [], [], [], [], [], []
    for c in chunks:
        sl, cum = sls[c], cums[c]
        lw = logw[sl]
        e_cum = jnp.exp(cum)
        e_inv = jnp.exp(-cum)
        e_end = jnp.exp(cum[CHUNK - 1:CHUNK, :] - cum)
        rh = r[sl] * e_cum
        r_hat.append(rh)
        bd_a.append(bd16(avec[sl] * jnp.exp(cum - lw)))
        bd_r.append(bd16(rh))
        bd_v.append(bd16(v[sl]))
        bd_be.append(bd16(bvec[sl] * e_end))
        bd_ke.append(bd16(k2[sl] * e_end))
        rhs.append(jnp.concatenate([tile4((bvec[sl] * e_inv).astype(BF16)),
                                    tile4((k2[sl] * e_inv).astype(BF16))], axis=0))
        p_end.append(e_cum[CHUNK - 1:CHUNK, :])
    lm = [_dot_nt(jnp.concatenate([bd_a[c], bd_r[c]], axis=0), rhs[c]) for c in chunks]
    l_pow = [lm[c][:GROUP_W, :GROUP_W] * strict32 for c in chunks]
    l_ak = [lm[c][:GROUP_W, GROUP_W:].astype(BF16) * strict16 for c in chunks]
    m_rb = [lm[c][GROUP_W:, :GROUP_W].astype(BF16) * incl16 for c in chunks]
    m_rk = [lm[c][GROUP_W:, GROUP_W:].astype(BF16) * incl16 for c in chunks]
    t_inv = [eye32 + l_pow[c] for c in chunks]
    for _ in range(5):
        lp16 = [l_pow[c].astype(BF16) for c in chunks]
        l_pow = [_dot(lp16[c], lp16[c]) for c in chunks]
        t_inv = [t_inv[c] + _dot(t_inv[c].astype(BF16), l_pow[c].astype(BF16)) for c in chunks]
    lakv = [_dot(l_ak[c], bd_v[c]).astype(BF16) for c in chunks]
    tw16 = [_dot(t_inv[c].astype(BF16), jnp.concatenate([bd_a[c], lakv[c]], axis=1)).astype(BF16)
            for c in chunks]
    x1 = [_dot_tn(bd_be[c], tw16[c]) for c in chunks]
    x2 = [_dot_tn(bd_ke[c], bd_v[c]) for c in chunks]
    mt = [_dot(m_rb[c], tw16[c]) for c in chunks]
    mv = [_dot(m_rk[c], bd_v[c]) for c in chunks]
    a_c = [(eye32 * p_end[c] + x1[c][:, :GROUP_W]).astype(BF16) for c in chunks]
    c_c = [x1[c][:, GROUP_W:] + x2[c] for c in chunks]
    q_c = [(jnp.where(same, tile4(r_hat[c]), 0.0) + mt[c][:, :GROUP_W]).astype(BF16)
           for c in chunks]
    y_loc = [mt[c][:, GROUP_W:] + mv[c] for c in chunks]
    ys = []
    state = state_ref[...]
    for c in chunks:
        s16 = state.astype(BF16)
        y_bd = _dot(q_c[c], s16) + y_loc[c]
        state = _dot(a_c[c], s16) + c_c[c]
        y_w = y_bd[0:CHUNK]
        for h in range(1, GROUP_HEADS):
            y_w = y_w + y_bd[h * CHUNK:(h + 1) * CHUNK]
        ys.append(y_w)
    state_ref[...] = state


    y = jnp.concatenate(ys, axis=0)
    mean = _split_dot(y, ones_bd) * (1.0 / HEAD_DIM)
    yc = y - mean
    var = _split_dot(yc * yc, ones_bd) * (1.0 / HEAD_DIM)
    yn = yc * lax.rsqrt(var + LNX_EPS) * lg_ref[...] + lb_ref[...]
    y_ref[...] = ((yn + bonus) * g).astype(y_ref.dtype)


def _rwkv(z, batch, seq, mu, decay_up, decay_w0, iclr_up, iclr_a0, gate_up, k_k, k_a, r_k,
          lnx_g, lnx_b, tm):
    m = z.shape[0]
    width = decay_up.shape[1]
    groups = width // GROUP_W
    nt = seq // tm
    lora_blk = 3 * groups

    def zspec(off):
        return pl.BlockSpec((tm, GROUP_W), lambda b, g, j: (b * nt + j, off(g)))

    def muspec(off):
        return pl.BlockSpec((1, GROUP_W), lambda b, g, j: (0, off(g)))

    def pspec(rows):
        return pl.BlockSpec((rows, GROUP_W), lambda b, g, j: (0, g))

    offs = [lambda g: g, lambda g: groups + g, lambda g: 2 * groups + g, lambda g: lora_blk]
    vec = lambda t: t.reshape(1, width)
    mu2 = mu.reshape(1, -1)
    return pl.pallas_call(
        functools.partial(_rwkv_kernel, tm=tm),
        grid=(batch, groups, nt),
        in_specs=[zspec(o) for o in offs] + [muspec(o) for o in offs] + [
            pspec(DECAY_LORA), pspec(1), pspec(AAA_LORA), pspec(1), pspec(GATE_LORA),
            pspec(1), pspec(1), pspec(1), pspec(1), pspec(1)],
        out_specs=pl.BlockSpec((tm, GROUP_W), lambda b, g, j: (b * nt + j, g)),
        out_shape=jax.ShapeDtypeStruct((m, width), BF16),
        scratch_shapes=[pltpu.VMEM((GROUP_W, GROUP_W), F32), pltpu.VMEM((8, GROUP_W), F32)],
        compiler_params=_cparams(("parallel", "parallel", "arbitrary")),
        name="rwkv7",
    )(z, z, z, z, mu2, mu2, mu2, mu2,
      decay_up.astype(BF16), vec(decay_w0), iclr_up.astype(BF16), vec(iclr_a0),
      gate_up.astype(BF16), vec(k_k), vec(k_a), vec(r_k), vec(lnx_g), vec(lnx_b))


def _attn_kernel(q_ref, kp_ref, kc_ref, vp_ref, vc_ref, bias_ref, o_ref, *, tq):
    j = pl.program_id(2)
    q = q_ref[...]
    kcat = jnp.concatenate([kp_ref[...], kc_ref[...]], axis=0)
    vcat = jnp.concatenate([vp_ref[...], vc_ref[...]], axis=0)
    lane = lax.broadcasted_iota(jnp.int32, (tq, 2 * HEAD_DIM), 1)
    col = lax.broadcasted_iota(jnp.int32, (tq, 2 * tq), 1)
    in_seq = (col >= tq) | (j > 0)
    outs = []
    for hh in range(2):
        mine = (lane // HEAD_DIM) == hh
        qm = jnp.where(mine, q, jnp.zeros_like(q)) * (HEAD_DIM ** -0.5)
        s = _dot_nt(qm.astype(BF16), kcat) + bias_ref[hh]
        s = jnp.where(in_seq, s, -jnp.inf)
        mx = jnp.max(s, axis=-1, keepdims=True)
        p = jnp.exp(s - mx)
        den = jnp.sum(p, axis=-1, keepdims=True)
        outs.append(_dot(p.astype(BF16), vcat) / den)
    o_ref[...] = jnp.where((lane // HEAD_DIM) == 0, outs[0], outs[1]).astype(o_ref.dtype)


def _attention(z_att, batch, seq, rel_bias, tq):
    m, w3 = z_att.shape
    width = w3 // 3
    heads = width // HEAD_DIM
    pairs = heads // 2
    nt = seq // tq
    assert tq == LEFT_CHUNKS * CHUNK
    band_w = (LEFT_CHUNKS + 1) * CHUNK
    qi = jnp.arange(CHUNK)[:, None] + LEFT_CHUNKS * CHUNK
    kj = jnp.arange(band_w)[None, :]
    rel = jnp.clip(qi - kj, -REL_CLIP, REL_CLIP) + REL_CLIP
    onehot = (rel[..., None] == jnp.arange(2 * REL_CLIP + 1)).astype(F32)
    band_bias = jnp.einsum("hr,qkr->hqk", rel_bias.astype(F32), onehot,
                           precision=lax.Precision.HIGHEST)
    bias = jnp.concatenate(
        [jnp.pad(band_bias, ((0, 0), (0, 0), (c * CHUNK, 2 * tq - band_w - c * CHUNK)),
                 constant_values=-jnp.inf) for c in range(tq // CHUNK)], axis=1)

    def spec(col_off, prev):
        if prev:
            return pl.BlockSpec((tq, 2 * HEAD_DIM),
                                lambda b, hp, j: (b * nt + jnp.maximum(j - 1, 0), col_off + hp))
        return pl.BlockSpec((tq, 2 * HEAD_DIM), lambda b, hp, j: (b * nt + j, col_off + hp))

    return pl.pallas_call(
        functools.partial(_attn_kernel, tq=tq),
        grid=(batch, pairs, nt),
        in_specs=[spec(0, False), spec(pairs, True), spec(pairs, False),
                  spec(2 * pairs, True), spec(2 * pairs, False),
                  pl.BlockSpec((2, tq, 2 * tq), lambda b, hp, j: (hp, 0, 0))],
        out_specs=pl.BlockSpec((tq, 2 * HEAD_DIM), lambda b, hp, j: (b * nt + j, hp)),
        out_shape=jax.ShapeDtypeStruct((m, width), BF16),
        compiler_params=_cparams(("parallel", "parallel", "parallel")),
        name="chunk_attention",
    )(z_att, z_att, z_att, z_att, z_att, bias)


def _out_proj_kernel(x_ref, yr_ref, ya_ref, w1_ref, w2_ref, o_ref):
    o_ref[...] = x_ref[...] + _dot(yr_ref[...], w1_ref[...]) + _dot(ya_ref[...], w2_ref[...])


def _out_proj(x, y_rwkv, y_att, w_out, tm):
    m, d = x.shape
    wr = y_rwkv.shape[1]
    wa = y_att.shape[1]
    w16 = w_out.astype(BF16)
    return pl.pallas_call(
        _out_proj_kernel,
        grid=(m // tm,),
        in_specs=[pl.BlockSpec((tm, d), lambda i: (i, 0)),
                  pl.BlockSpec((tm, wr), lambda i: (i, 0)),
                  pl.BlockSpec((tm, wa), lambda i: (i, 0)),
                  pl.BlockSpec((wr, d), lambda i: (0, 0)),
                  pl.BlockSpec((wa, d), lambda i: (0, 0))],
        out_specs=pl.BlockSpec((tm, d), lambda i: (i, 0)),
        out_shape=jax.ShapeDtypeStruct((m, d), F32),
        compiler_params=_cparams(("parallel",)),
        name="out_proj",
    )(x, y_rwkv, y_att, w16[:wr], w16[wr:])


def _peer_query_kernel(h_ref, g_ref, w_ref, q_ref, xnt_ref):
    xn = _rms(h_ref[...], g_ref[...])
    xnt_ref[...] = xn.T.astype(BF16)
    q_ref[...] = _dot(xn.astype(BF16), w_ref[...]).astype(q_ref.dtype)


def _peer_query(h, g, wq, tm):
    m, d = h.shape
    n = wq.shape[1]
    return pl.pallas_call(
        _peer_query_kernel,
        grid=(m // tm,),
        in_specs=[pl.BlockSpec((tm, d), lambda i: (i, 0)),
                  pl.BlockSpec((1, d), lambda i: (0, 0)),
                  pl.BlockSpec((d, n), lambda i: (0, 0))],
        out_specs=[pl.BlockSpec((tm, n), lambda i: (i, 0)),
                   pl.BlockSpec((d, tm), lambda i: (0, i))],
        out_shape=[jax.ShapeDtypeStruct((m, n), BF16), jax.ShapeDtypeStruct((d, m), BF16)],
        compiler_params=_cparams(("parallel",)),
        name="peer_query",
    )(h, g.reshape(1, d), wq.astype(BF16))


NOT_TOP = 127.0


def _top_values(s, count):
    rows = lax.broadcasted_iota(jnp.int32, (count, s.shape[1]), 0)
    tops = jnp.zeros((count, s.shape[1]), F32)
    rank = jnp.full(s.shape, NOT_TOP, F32)
    cur = s
    for i in range(count):
        mx = jnp.max(cur, axis=0, keepdims=True)
        hit = cur == mx
        tops = jnp.where(rows == i, mx, tops)
        rank = jnp.where(hit, float(i), rank)
        cur = jnp.where(hit, -jnp.inf, cur)
    return tops, rank


def _candidate_sums(top0, top1):
    half = PEER_TOPK // 2
    row16 = lax.broadcasted_iota(jnp.int32, top0.shape, 0)
    row8 = lax.broadcasted_iota(jnp.int32, (half, top0.shape[1]), 0)
    ninf = -jnp.inf
    pieces = [top0 + top1[0:1, :],
              jnp.where(row16 >= 1, top0[0:1, :] + top1, ninf)]
    for i in range(1, 4):
        pieces.append(jnp.where(row8 >= i, top0[:half] + top1[i:i + 1, :], ninf))
        if i < 3:
            pieces.append(jnp.where(row8 > i, top0[i:i + 1, :] + top1[:half], ninf))
    return pieces


def _peer_score_kernel(q_ref, keys_ref, lrow_ref, e0_ref, r1_ref, e1_ref):
    for h in range(PEER_HEADS):
        st = []
        for half in range(2):
            c0 = (2 * h + half) * PEER_HALF
            st.append(_dot_nt(keys_ref[2 * h + half], q_ref[:, c0:c0 + PEER_HALF]))
        top0, _ = _top_values(st[0], PEER_TOPK)
        top1, rank1 = _top_values(st[1], PEER_TOPK)
        cands = _candidate_sums(top0, top1)
        cur = list(cands)
        thr = None
        for _ in range(PEER_TOPK):
            big = jnp.maximum(cur[0], cur[1])
            small = cur[2]
            for c in cur[3:]:
                small = jnp.maximum(small, c)
            thr = jnp.maximum(jnp.max(big, axis=0, keepdims=True), jnp.max(small, axis=0, keepdims=True))
            cur = [jnp.where(c == thr, -jnp.inf, c) for c in cur]
        best = top0[0:1, :] + top1[0:1, :]
        den = jnp.zeros_like(thr)
        for c in cands:
            den = den + jnp.sum(jnp.where(c >= thr, jnp.exp(c - best), 0.0), axis=0, keepdims=True)
        lrow = jnp.zeros_like(st[0])
        for b in range(PEER_TOPK):
            lrow = lrow + jnp.where(st[0] + top1[b:b + 1, :] >= thr, 1.0, 0.0)
        lrow_ref[h] = lrow
        e0_ref[h] = jnp.exp(st[0] - top0[0:1, :]) * (0.5 / den)
        r1_ref[h] = rank1.astype(r1_ref.dtype)
        e1_ref[h] = jnp.exp(st[1] - top1[0:1, :]).astype(e1_ref.dtype)


def _peer_scores(q, sub_keys, tb):
    m = q.shape[0]
    keys = sub_keys.reshape(2 * PEER_HEADS, N_KEYS, PEER_HALF).astype(BF16)
    slab_spec = pl.BlockSpec((PEER_HEADS, N_KEYS, tb), lambda i: (0, 0, i))
    slab = lambda dt: jax.ShapeDtypeStruct((PEER_HEADS, N_KEYS, m), dt)
    return pl.pallas_call(
        _peer_score_kernel,
        grid=(m // tb,),
        in_specs=[pl.BlockSpec((tb, q.shape[1]), lambda i: (i, 0)),
                  pl.BlockSpec(keys.shape, lambda i: (0, 0, 0))],
        out_specs=[slab_spec, slab_spec, slab_spec, slab_spec],
        out_shape=[slab(F32), slab(F32), slab(BF16), slab(BF16)],
        compiler_params=_cparams(("parallel",)),
        name="peer_scores",
    )(q, keys)


def _gelu_tanh_x2(x):
    c = 0.7978845608028654
    return x * (1.0 + jnp.tanh(x * (c + (c * 0.044715) * (x * x))))


LANES = 128
SUBLANES = 8


def _peer_dense_kernel(xnt_ref, u_ref, vt_ref, lrow_ref, e0_ref, r1_ref, e1_ref,
                       ot_ref, act_ref, w_ref, *, rows_per_step, n_blocks):
    e = pl.program_id(1)
    tb = xnt_ref.shape[1]
    pack = 2 * SUBLANES

    @pl.when(e == 0)
    def _():
        ot_ref[...] = jnp.zeros_like(ot_ref)
        act_ref[...] = jnp.zeros_like(act_ref)
        w_ref[...] = jnp.zeros_like(w_ref)

    cur = e % 2
    prv = 1 - cur
    act_ref[cur] = _dot(u_ref[...], xnt_ref[...])
    ot_ref[...] += _dot(vt_ref[...], w_ref[cur])

    blk = jnp.clip(e - 1, 0, n_blocks - 1)
    zero = jnp.zeros((pack, LANES), BF16)
    for r in range(rows_per_step):
        i0 = blk * rows_per_step + r
        lrows = [lrow_ref[h, pl.ds(i0, 1), :] for h in range(PEER_HEADS)]
        e0rows = [e0_ref[h, pl.ds(i0, 1), :] for h in range(PEER_HEADS)]
        for tc in range(tb // LANES):
            lanes = slice(tc * LANES, (tc + 1) * LANES)
            l_b = [jnp.broadcast_to(lrows[h][:, lanes], (pack, LANES)).astype(BF16)
                   for h in range(PEER_HEADS)]
            e0_b = [jnp.broadcast_to(e0rows[h][:, lanes], (pack, LANES)).astype(BF16)
                    for h in range(PEER_HEADS)]
            for ib in range(N_KEYS // pack):
                krows = slice(ib * pack, (ib + 1) * pack)
                gate = None
                for h in range(PEER_HEADS):
                    term = jnp.where(r1_ref[h, krows, lanes] < l_b[h], e1_ref[h, krows, lanes], zero) * e0_b[h]
                    gate = term if gate is None else gate + term
                rows = slice(r * N_KEYS + ib * pack, r * N_KEYS + (ib + 1) * pack)
                w_ref[prv, rows, lanes] = gate * _gelu_tanh_x2(act_ref[prv, rows, lanes]).astype(BF16)


def _peer_dense(xnt, u16, vt16, lrow, e0, r1, e1, tb, eb):
    d, m = xnt.shape
    n_blocks = u16.shape[0] // eb
    slab_spec = pl.BlockSpec((PEER_HEADS, N_KEYS, tb), lambda i, e: (0, 0, i))
    return pl.pallas_call(
        functools.partial(_peer_dense_kernel, rows_per_step=eb // N_KEYS, n_blocks=n_blocks),
        grid=(m // tb, n_blocks + 2),
        in_specs=[pl.BlockSpec((d, tb), lambda i, e: (0, i)),
                  pl.BlockSpec((eb, d), lambda i, e: (jnp.minimum(e, n_blocks - 1), 0)),
                  pl.BlockSpec((d, eb), lambda i, e: (0, jnp.clip(e - 2, 0, n_blocks - 1))),
                  slab_spec, slab_spec, slab_spec, slab_spec],
        out_specs=pl.BlockSpec((d, tb), lambda i, e: (0, i)),
        out_shape=jax.ShapeDtypeStruct((d, m), F32),
        scratch_shapes=[pltpu.VMEM((2, eb, tb), F32), pltpu.VMEM((2, eb, tb), BF16)],
        compiler_params=_cparams(("parallel", "arbitrary")),
        name="peer_dense",
    )(xnt, u16, vt16, lrow, e0, r1, e1)


def _ple_kernel(h_ref, po_ref, g_ref, wg_ref, p_ref, wu_ref, fg_ref, o_ref):
    h = h_ref[...] + po_ref[...].T
    xn = _rms(h, g_ref[...]).astype(BF16)
    gate = _sigmoid(_dot(xn, wg_ref[...]))
    up = _dot(p_ref[...].astype(BF16), wu_ref[...])
    o_ref[...] = _rms(h + gate * up, fg_ref[...])


def _ple_final(h, peer_out, g, w_gate, p, w_up, final_g, tm):
    m, d = h.shape
    pd = p.shape[1]
    return pl.pallas_call(
        _ple_kernel,
        grid=(m // tm,),
        in_specs=[pl.BlockSpec((tm, d), lambda i: (i, 0)),
                  pl.BlockSpec((d, tm), lambda i: (0, i)),
                  pl.BlockSpec((1, d), lambda i: (0, 0)),
                  pl.BlockSpec((d, d), lambda i: (0, 0)),
                  pl.BlockSpec((tm, pd), lambda i: (i, 0)),
                  pl.BlockSpec((pd, d), lambda i: (0, 0)),
                  pl.BlockSpec((1, d), lambda i: (0, 0))],
        out_specs=pl.BlockSpec((tm, d), lambda i: (i, 0)),
        out_shape=jax.ShapeDtypeStruct((m, d), F32),
        compiler_params=_cparams(("parallel",)),
        name="ple_final",
    )(h, peer_out, g.reshape(1, d), w_gate.astype(BF16), p, w_up.astype(BF16),
      final_g.reshape(1, d))


def _pick(n, prefs):
    for t in prefs:
        if n % t == 0:
            return t
    return n


def kernel(x, p, mix_norm_g, w_in, rwkv_mu, decay_up, decay_w0, iclr_up, iclr_a0, gate_up, k_k, k_a, r_k, lnx_g, lnx_b, rel_bias, w_out, ffn_norm_g, peer_wq, peer_sub_keys, peer_u, peer_v, ple_norm_g, ple_gate_w, ple_up, final_norm_g):
    batch, seq, d = x.shape
    m = batch * seq
    assert w_in.shape[0] == 1, "the final norm is fused into the (single) layer's last stage"
    rwkv_w = decay_up.shape[2]
    rwkv_cols = rwkv_mu.shape[1]
    h = x.reshape(m, d)
    for i in range(1):
        w16 = w_in[i].astype(BF16)
        tm = _pick(m, (512, 256, 128))
        z_rwkv = _norm_matmul(h, mix_norm_g[i], w16[:, :rwkv_cols], F32, tm,
                              _pick(rwkv_cols, (1664, 1024, 512, 256, 128)))
        z_att = _norm_matmul(h, mix_norm_g[i], w16[:, rwkv_cols:], BF16, tm,
                             _pick(w16.shape[1] - rwkv_cols, (1536, 1024, 512, 256, 128)))
        y_rwkv = _rwkv(z_rwkv, batch, seq, rwkv_mu[i], decay_up[i], decay_w0[i], iclr_up[i],
                       iclr_a0[i], gate_up[i], k_k[i], k_a[i], r_k[i].reshape(rwkv_w),
                       lnx_g[i], lnx_b[i], _pick(seq, (512, 256, 128, 64)))
        y_att = _attention(z_att, batch, seq, rel_bias[i], _pick(seq, (512,)))
        h1 = _out_proj(h, y_rwkv, y_att, w_out[i], _pick(m, (256, 128)))
        q, xnt = _peer_query(h1, ffn_norm_g[i], peer_wq[i], _pick(m, (256, 128)))
        lrow, e0, r1, e1 = _peer_scores(q, peer_sub_keys[i], _pick(m, (256, 128)))
        peer_out_t = _peer_dense(xnt, peer_u[i].astype(BF16), peer_v[i].T.astype(BF16), lrow, e0, r1,
                                 e1, _pick(m, (512, 256, 128)), 4 * N_KEYS)
        h = _ple_final(h1, peer_out_t, ple_norm_g[i], ple_gate_w[i], p[i].reshape(m, -1), ple_up[i],
                       final_norm_g, _pick(m, (256, 128)))
    return h.reshape(batch, seq, d)
```

```python
import functools

import jax
import jax.numpy as jnp
from jax import lax
from jax.experimental import pallas as pl
from jax.experimental.pallas import tpu as pltpu

F32 = jnp.float32
BF16 = jnp.bfloat16

HEAD_DIM = 64
CHUNK = 64
LEFT_CHUNKS = 8
REL_CLIP = 128
DECAY_LORA = 64
AAA_LORA = 64
GATE_LORA = 128
PEER_HEADS = 8
N_KEYS = 128
PEER_HALF = 128
PEER_TOPK = 16
NORM_EPS = 1e-6
LNX_EPS = 64e-5

VMEM_LIMIT_BYTES = 56 * 1024 * 1024
GROUP_HEADS = 4
GROUP_W = GROUP_HEADS * HEAD_DIM


def _cparams(sem, flags=None):
    return pltpu.CompilerParams(dimension_semantics=sem, vmem_limit_bytes=VMEM_LIMIT_BYTES,
                                flags=flags)


def _dot(a, b):
    return jnp.dot(a, b, preferred_element_type=F32)


def _dot_nt(a, b):
    return lax.dot_general(a, b, (((1,), (1,)), ((), ())), preferred_element_type=F32)


def _dot_tn(a, b):
    return lax.dot_general(a, b, (((0,), (0,)), ((), ())), preferred_element_type=F32)


def _rms(x, g):
    return x * lax.rsqrt(jnp.mean(x * x, axis=-1, keepdims=True) + NORM_EPS) * g


def _sigmoid(x):
    return 1.0 / (1.0 + jnp.exp(-x))


def _norm_matmul_kernel(x_ref, g_ref, w_ref, o_ref, xn_ref):
    @pl.when(pl.program_id(1) == 0)
    def _():
        xn_ref[...] = _rms(x_ref[...], g_ref[...]).astype(BF16)

    o_ref[...] = _dot(xn_ref[...], w_ref[...]).astype(o_ref.dtype)


def _norm_matmul(x, g, w, out_dtype, tm, tn):
    m, d = x.shape
    n = w.shape[1]
    return pl.pallas_call(
        _norm_matmul_kernel,
        grid=(m // tm, n // tn),
        in_specs=[
            pl.BlockSpec((tm, d), lambda i, j: (i, 0)),
            pl.BlockSpec((1, d), lambda i, j: (0, 0)),
            pl.BlockSpec((d, tn), lambda i, j: (0, j)),
        ],
        out_specs=pl.BlockSpec((tm, tn), lambda i, j: (i, j)),
        out_shape=jax.ShapeDtypeStruct((m, n), out_dtype),
        scratch_shapes=[pltpu.VMEM((tm, d), BF16)],
        compiler_params=_cparams(("parallel", "arbitrary")),
        name="norm_matmul",
    )(x, g.reshape(1, d), w)


def _split_dot(x, w_bf16):
    hi = x.astype(BF16)
    lo = (x - hi.astype(F32)).astype(BF16)
    return _dot(hi, w_bf16) + _dot(lo, w_bf16)


def _rwkv_kernel(zr_ref, zk_ref, zv_ref, zl_ref, mur_ref, muk_ref, muv_ref, mul_ref,
                 dup_ref, dw0_ref, iup_ref, ia0_ref, gup_ref, kk_ref, ka_ref, rk_ref,
                 lg_ref, lb_ref, y_ref, state_ref, last_ref, *, tm):
    j = pl.program_id(2)
    n_chunks = tm // CHUNK

    @pl.when(j == 0)
    def _():
        state_ref[...] = jnp.zeros_like(state_ref)
        last_ref[...] = jnp.zeros_like(last_ref)

    row = lax.broadcasted_iota(jnp.int32, (tm, GROUP_W), 0)

    def shift_mix(ref, mu_ref, slot):
        cur = ref[...]
        prev = pltpu.roll(cur, 1, axis=0)
        prev = jnp.where(row == 0, last_ref[slot:slot + 1, :], prev)
        last_ref[slot:slot + 1, :] = cur[tm - 1:tm, :]
        return cur + (prev - cur) * mu_ref[...]

    r = shift_mix(zr_ref, mur_ref, 0)
    k = shift_mix(zk_ref, muk_ref, 1)
    v = shift_mix(zv_ref, muv_ref, 2)
    lo = shift_mix(zl_ref, mul_ref, 3)
    wd = lo[:, :DECAY_LORA]
    ad = lo[:, DECAY_LORA:DECAY_LORA + AAA_LORA]
    gd = lo[:, DECAY_LORA + AAA_LORA:]

    dec = dw0_ref[...] + _dot(jnp.tanh(wd).astype(BF16), dup_ref[...])
    neg = -dec
    softplus = jnp.maximum(neg, 0.0) + jnp.log(1.0 + jnp.exp(-jnp.abs(neg)))
    logw = -jnp.exp(-softplus - 0.5)
    a = _sigmoid(ia0_ref[...] + _dot(ad.astype(BF16), iup_ref[...]))
    g = _dot(_sigmoid(gd).astype(BF16), gup_ref[...])

    hrow = lax.broadcasted_iota(jnp.int32, (GROUP_W, GROUP_W), 0)
    hcol = lax.broadcasted_iota(jnp.int32, (GROUP_W, GROUP_W), 1)
    same = (hrow // HEAD_DIM) == (hcol // HEAD_DIM)
    ones_bd = jnp.where(same, 1.0, 0.0).astype(BF16)

    kkr = k * kk_ref[...]
    kk_norm = jnp.sqrt(_split_dot(kkr * kkr, ones_bd))
    kk = kkr / jnp.maximum(kk_norm, 1e-12)
    k2 = k * (1.0 + (a - 1.0) * ka_ref[...])
    bonus = _split_dot(r * k2 * rk_ref[...], ones_bd) * v
    avec = -kk
    bvec = kk * a

    t_row = hrow % CHUNK
    t_col = hcol % CHUNK
    strict = same & (t_row > t_col)
    incl = same & (t_row >= t_col)
    eye = hrow == hcol
    tri = jnp.where(lax.broadcasted_iota(jnp.int32, (CHUNK, CHUNK), 0)
                    >= lax.broadcasted_iota(jnp.int32, (CHUNK, CHUNK), 1), 1.0, 0.0).astype(BF16)

    def tile4(x):
        return jnp.concatenate([x] * GROUP_HEADS, axis=0)

    def bd16(x):
        return tile4(x.astype(BF16)) * ones_bd

    strict16 = jnp.where(strict, 1.0, 0.0).astype(BF16)
    incl16 = jnp.where(incl, 1.0, 0.0).astype(BF16)
    strict32 = jnp.where(strict, 1.0, 0.0)
    eye32 = jnp.where(eye, 1.0, 0.0)
    chunks = range(n_chunks)
    sls = [slice(c * CHUNK, (c + 1) * CHUNK) for c in chunks]

    cums = []
    for c in chunks:
        lw = logw[sls[c]]
        lw_hi = lw.astype(BF16)
        rem = lw - lw_hi.astype(F32)
        lw_mid = rem.astype(BF16)
        lw_lo = (rem - lw_mid.astype(F32)).astype(BF16)
        cums.append(_dot(tri, lw_hi) + _dot(tri, lw_mid) + _dot(tri, lw_lo))
    bd_a, bd_r, bd_v, bd_be, bd_ke, rhs, p_end, r_hat = [], [], [], [], [], [], [], []
    for c in chunks:
        sl, cum = sls[c], cums[c]
        lw = logw[sl]
        e_cum = jnp.exp(cum)
        e_inv = jnp.exp(-cum)
        e_end = jnp.exp(cum[CHUNK - 1:CHUNK, :] - cum)
        rh = r[sl] * e_cum
        r_hat.append(rh)
        bd_a.append(bd16(avec[sl] * jnp.exp(cum - lw)))
        bd_r.append(bd16(rh))
        bd_v.append(bd16(v[sl]))
        bd_be.append(bd16(bvec[sl] * e_end))
        bd_ke.append(bd16(k2[sl] * e_end))
        rhs.append(jnp.concatenate([tile4((bvec[sl] * e_inv).astype(BF16)),
                                    tile4((k2[sl] * e_inv).astype(BF16))], axis=0))
        p_end.append(e_cum[CHUNK - 1:CHUNK, :])
    lm = [_dot_nt(jnp.concatenate([bd_a[c], bd_r[c]], axis=0), rhs[c]) for c in chunks]
    l_pow = [lm[c][:GROUP_W, :GROUP_W] * strict32 for c in chunks]
    l_ak = [lm[c][:GROUP_W, GROUP_W:].astype(BF16) * strict16 for c in chunks]
    m_rb = [lm[c][GROUP_W:, :GROUP_W].astype(BF16) * incl16 for c in chunks]
    m_rk = [lm[c][GROUP_W:, GROUP_W:].astype(BF16) * incl16 for c in chunks]
    t_inv = [eye32 + l_pow[c] for c in chunks]
    for _ in range(5):
        lp16 = [l_pow[c].astype(BF16) for c in chunks]
        l_pow = [_dot(lp16[c], lp16[c]) for c in chunks]
        t_inv = [t_inv[c] + _dot(t_inv[c].astype(BF16), l_pow[c].astype(BF16)) for c in chunks]
    lakv = [_dot(l_ak[c], bd_v[c]).astype(BF16) for c in chunks]
    tw16 = [_dot(t_inv[c].astype(BF16), jnp.concatenate([bd_a[c], lakv[c]], axis=1)).astype(BF16)
            for c in chunks]
    x1 = [_dot_tn(bd_be[c], tw16[c]) for c in chunks]
    x2 = [_dot_tn(bd_ke[c], bd_v[c]) for c in chunks]
    mt = [_dot(m_rb[c], tw16[c]) for c in chunks]
    mv = [_dot(m_rk[c], bd_v[c]) for c in chunks]
    a_c = [(eye32 * p_end[c] + x1[c][:, :GROUP_W]).astype(BF16) for c in chunks]
    c_c = [x1[c][:, GROUP_W:] + x2[c] for c in chunks]
    q_c = [(jnp.where(same, tile4(r_hat[c]), 0.0) + mt[c][:, :GROUP_W]).astype(BF16)
           for c in chunks]
    y_loc = [mt[c][:, GROUP_W:] + mv[c] for c in chunks]
    ys = []
    state = state_ref[...]
    for c in chunks:
        s16 = state.astype(BF16)
        y_bd = _dot(q_c[c], s16) + y_loc[c]
        state = _dot(a_c[c], s16) + c_c[c]
        y_w = y_bd[0:CHUNK]
        for h in range(1, GROUP_HEADS):
            y_w = y_w + y_bd[h * CHUNK:(h + 1) * CHUNK]
        ys.append(y_w)
    state_ref[...] = state


    y = jnp.concatenate(ys, axis=0)
    mean = _split_dot(y, ones_bd) * (1.0 / HEAD_DIM)
    yc = y - mean
    var = _split_dot(yc * yc, ones_bd) * (1.0 / HEAD_DIM)
    yn = yc * lax.rsqrt(var + LNX_EPS) * lg_ref[...] + lb_ref[...]
    y_ref[...] = ((yn + bonus) * g).astype(y_ref.dtype)


def _rwkv(z, batch, seq, mu, decay_up, decay_w0, iclr_up, iclr_a0, gate_up, k_k, k_a, r_k,
          lnx_g, lnx_b, tm):
    m = z.shape[0]
    width = decay_up.shape[1]
    groups = width // GROUP_W
    nt = seq // tm
    lora_blk = 3 * groups

    def zspec(off):
        return pl.BlockSpec((tm, GROUP_W), lambda b, g, j: (b * nt + j, off(g)))

    def muspec(off):
        return pl.BlockSpec((1, GROUP_W), lambda b, g, j: (0, off(g)))

    def pspec(rows):
        return pl.BlockSpec((rows, GROUP_W), lambda b, g, j: (0, g))

    offs = [lambda g: g, lambda g: groups + g, lambda g: 2 * groups + g, lambda g: lora_blk]
    vec = lambda t: t.reshape(1, width)
    mu2 = mu.reshape(1, -1)
    return pl.pallas_call(
        functools.partial(_rwkv_kernel, tm=tm),
        grid=(batch, groups, nt),
        in_specs=[zspec(o) for o in offs] + [muspec(o) for o in offs] + [
            pspec(DECAY_LORA), pspec(1), pspec(AAA_LORA), pspec(1), pspec(GATE_LORA),
            pspec(1), pspec(1), pspec(1), pspec(1), pspec(1)],
        out_specs=pl.BlockSpec((tm, GROUP_W), lambda b, g, j: (b * nt + j, g)),
        out_shape=jax.ShapeDtypeStruct((m, width), BF16),
        scratch_shapes=[pltpu.VMEM((GROUP_W, GROUP_W), F32), pltpu.VMEM((8, GROUP_W), F32)],
        compiler_params=_cparams(("parallel", "parallel", "arbitrary")),
        name="rwkv7",
    )(z, z, z, z, mu2, mu2, mu2, mu2,
      decay_up.astype(BF16), vec(decay_w0), iclr_up.astype(BF16), vec(iclr_a0),
      gate_up.astype(BF16), vec(k_k), vec(k_a), vec(r_k), vec(lnx_g), vec(lnx_b))


def _attn_kernel(q_ref, kp_ref, kc_ref, vp_ref, vc_ref, bias_ref, o_ref, *, tq):
    j = pl.program_id(2)
    q = q_ref[...]
    kcat = jnp.concatenate([kp_ref[...], kc_ref[...]], axis=0)
    vcat = jnp.concatenate([vp_ref[...], vc_ref[...]], axis=0)
    lane = lax.broadcasted_iota(jnp.int32, (tq, 2 * HEAD_DIM), 1)
    col = lax.broadcasted_iota(jnp.int32, (tq, 2 * tq), 1)
    in_seq = (col >= tq) | (j > 0)
    outs = []
    for hh in range(2):
        mine = (lane // HEAD_DIM) == hh
        qm = jnp.where(mine, q, jnp.zeros_like(q)) * (HEAD_DIM ** -0.5)
        s = _dot_nt(qm.astype(BF16), kcat) + bias_ref[hh]
        s = jnp.where(in_seq, s, -jnp.inf)
        mx = jnp.max(s, axis=-1, keepdims=True)
        p = jnp.exp(s - mx)
        den = jnp.sum(p, axis=-1, keepdims=True)
        outs.append(_dot(p.astype(BF16), vcat) / den)
    o_ref[...] = jnp.where((lane // HEAD_DIM) == 0, outs[0], outs[1]).astype(o_ref.dtype)


def _attention(z_att, batch, seq, rel_bias, tq):
    m, w3 = z_att.shape
    width = w3 // 3
    heads = width // HEAD_DIM
    pairs = heads // 2
    nt = seq // tq
    assert tq == LEFT_CHUNKS * CHUNK
    band_w = (LEFT_CHUNKS + 1) * CHUNK
    qi = jnp.arange(CHUNK)[:, None] + LEFT_CHUNKS * CHUNK
    kj = jnp.arange(band_w)[None, :]
    rel = jnp.clip(qi - kj, -REL_CLIP, REL_CLIP) + REL_CLIP
    onehot = (rel[..., None] == jnp.arange(2 * REL_CLIP + 1)).astype(F32)
    band_bias = jnp.einsum("hr,qkr->hqk", rel_bias.astype(F32), onehot,
                           precision=lax.Precision.HIGHEST)
    bias = jnp.concatenate(
        [jnp.pad(band_bias, ((0, 0), (0, 0), (c * CHUNK, 2 * tq - band_w - c * CHUNK)),
                 constant_values=-jnp.inf) for c in range(tq // CHUNK)], axis=1)

    def spec(col_off, prev):
        if prev:
            return pl.BlockSpec((tq, 2 * HEAD_DIM),
                                lambda b, hp, j: (b * nt + jnp.maximum(j - 1, 0), col_off + hp))
        return pl.BlockSpec((tq, 2 * HEAD_DIM), lambda b, hp, j: (b * nt + j, col_off + hp))

    return pl.pallas_call(
        functools.partial(_attn_kernel, tq=tq),
        grid=(batch, pairs, nt),
        in_specs=[spec(0, False), spec(pairs, True), spec(pairs, False),
                  spec(2 * pairs, True), spec(2 * pairs, False),
                  pl.BlockSpec((2, tq, 2 * tq), lambda b, hp, j: (hp, 0, 0))],
        out_specs=pl.BlockSpec((tq, 2 * HEAD_DIM), lambda b, hp, j: (b * nt + j, hp)),
        out_shape=jax.ShapeDtypeStruct((m, width), BF16),
        compiler_params=_cparams(("parallel", "parallel", "parallel")),
        name="chunk_attention",
    )(z_att, z_att, z_att, z_att, z_att, bias)


def _out_proj_kernel(x_ref, yr_ref, ya_ref, w1_ref, w2_ref, o_ref):
    o_ref[...] = x_ref[...] + _dot(yr_ref[...], w1_ref[...]) + _dot(ya_ref[...], w2_ref[...])


def _out_proj(x, y_rwkv, y_att, w_out, tm):
    m, d = x.shape
    wr = y_rwkv.shape[1]
    wa = y_att.shape[1]
    w16 = w_out.astype(BF16)
    return pl.pallas_call(
        _out_proj_kernel,
        grid=(m // tm,),
        in_specs=[pl.BlockSpec((tm, d), lambda i: (i, 0)),
                  pl.BlockSpec((tm, wr), lambda i: (i, 0)),
                  pl.BlockSpec((tm, wa), lambda i: (i, 0)),
                  pl.BlockSpec((wr, d), lambda i: (0, 0)),
                  pl.BlockSpec((wa, d), lambda i: (0, 0))],
        out_specs=pl.BlockSpec((tm, d), lambda i: (i, 0)),
        out_shape=jax.ShapeDtypeStruct((m, d), F32),
        compiler_params=_cparams(("parallel",)),
        name="out_proj",
    )(x, y_rwkv, y_att, w16[:wr], w16[wr:])


def _peer_query_kernel(h_ref, g_ref, w_ref, q_ref, xnt_ref):
    xn = _rms(h_ref[...], g_ref[...])
    xnt_ref[...] = xn.T.astype(BF16)
    q_ref[...] = _dot(xn.astype(BF16), w_ref[...]).astype(q_ref.dtype)


def _peer_query(h, g, wq, tm):
    m, d = h.shape
    n = wq.shape[1]
    return pl.pallas_call(
        _peer_query_kernel,
        grid=(m // tm,),
        in_specs=[pl.BlockSpec((tm, d), lambda i: (i, 0)),
                  pl.BlockSpec((1, d), lambda i: (0, 0)),
                  pl.BlockSpec((d, n), lambda i: (0, 0))],
        out_specs=[pl.BlockSpec((tm, n), lambda i: (i, 0)),
                   pl.BlockSpec((d, tm), lambda i: (0, i))],
        out_shape=[jax.ShapeDtypeStruct((m, n), BF16), jax.ShapeDtypeStruct((d, m), BF16)],
        compiler_params=_cparams(("parallel",)),
        name="peer_query",
    )(h, g.reshape(1, d), wq.astype(BF16))


NOT_TOP = 127.0


def _top_values(s, count):
    rows = lax.broadcasted_iota(jnp.int32, (count, s.shape[1]), 0)
    tops = jnp.zeros((count, s.shape[1]), F32)
    rank = jnp.full(s.shape, NOT_TOP, F32)
    cur = s
    for i in range(count):
        mx = jnp.max(cur, axis=0, keepdims=True)
        hit = cur == mx
        tops = jnp.where(rows == i, mx, tops)
        rank = jnp.where(hit, float(i), rank)
        cur = jnp.where(hit, -jnp.inf, cur)
    return tops, rank


def _candidate_sums(top0, top1):
    half = PEER_TOPK // 2
    row16 = lax.broadcasted_iota(jnp.int32, top0.shape, 0)
    row8 = lax.broadcasted_iota(jnp.int32, (half, top0.shape[1]), 0)
    ninf = -jnp.inf
    pieces = [top0 + top1[0:1, :],
              jnp.where(row16 >= 1, top0[0:1, :] + top1, ninf)]
    for i in range(1, 4):
        pieces.append(jnp.where(row8 >= i, top0[:half] + top1[i:i + 1, :], ninf))
        if i < 3:
            pieces.append(jnp.where(row8 > i, top0[i:i + 1, :] + top1[:half], ninf))
    return pieces


def _peer_score_kernel(q_ref, keys_ref, lrow_ref, e0_ref, r1_ref, e1_ref):
    for h in range(PEER_HEADS):
        st = []
        for half in range(2):
            c0 = (2 * h + half) * PEER_HALF
            st.append(_dot_nt(keys_ref[2 * h + half], q_ref[:, c0:c0 + PEER_HALF]))
        top0, _ = _top_values(st[0], PEER_TOPK)
        top1, rank1 = _top_values(st[1], PEER_TOPK)
        cands = _candidate_sums(top0, top1)
        cur = list(cands)
        thr = None
        for _ in range(PEER_TOPK):
            big = jnp.maximum(cur[0], cur[1])
            small = cur[2]
            for c in cur[3:]:
                small = jnp.maximum(small, c)
            thr = jnp.maximum(jnp.max(big, axis=0, keepdims=True), jnp.max(small, axis=0, keepdims=True))
            cur = [jnp.where(c == thr, -jnp.inf, c) for c in cur]
        best = top0[0:1, :] + top1[0:1, :]
        den = jnp.zeros_like(thr)
        for c in cands:
            den = den + jnp.sum(jnp.where(c >= thr, jnp.exp(c - best), 0.0), axis=0, keepdims=True)
        lrow = jnp.zeros_like(st[0])
        for b in range(PEER_TOPK):
            lrow = lrow + jnp.where(st[0] + top1[b:b + 1, :] >= thr, 1.0, 0.0)
        lrow_ref[h] = lrow
        e0_ref[h] = jnp.exp(st[0] - top0[0:1, :]) * (0.5 / den)
        r1_ref[h] = pltpu.bitcast(rank1.astype(BF16), jnp.uint32)
        e1_ref[h] = pltpu.bitcast(jnp.exp(st[1] - top1[0:1, :]).astype(BF16), jnp.uint32)


def _peer_scores(q, sub_keys, tb):
    m = q.shape[0]
    keys = sub_keys.reshape(2 * PEER_HEADS, N_KEYS, PEER_HALF).astype(BF16)
    slab_spec = pl.BlockSpec((PEER_HEADS, N_KEYS, tb), lambda i: (0, 0, i))
    slab = lambda dt: jax.ShapeDtypeStruct((PEER_HEADS, N_KEYS, m), dt)
    pack_spec = pl.BlockSpec((PEER_HEADS, N_KEYS // 2, tb), lambda i: (0, 0, i))
    packed = jax.ShapeDtypeStruct((PEER_HEADS, N_KEYS // 2, m), jnp.uint32)
    return pl.pallas_call(
        _peer_score_kernel,
        grid=(m // tb,),
        in_specs=[pl.BlockSpec((tb, q.shape[1]), lambda i: (i, 0)),
                  pl.BlockSpec(keys.shape, lambda i: (0, 0, 0))],
        out_specs=[slab_spec, slab_spec, pack_spec, pack_spec],
        out_shape=[slab(F32), slab(F32), packed, packed],
        compiler_params=_cparams(("parallel",)),
        name="peer_scores",
    )(q, keys)


def _gelu_tanh_x2(x):
    c = 0.7978845608028654
    return x * (1.0 + jnp.tanh(x * (c + (c * 0.044715) * (x * x))))


LANES = 128
SUBLANES = 8


def _peer_dense_kernel(xnt_ref, u_ref, vt_ref, lrow_ref, e0_ref, r1_ref, e1_ref,
                       ot_ref, act_ref, w_ref, *, rows_per_step, n_blocks):
    e = pl.program_id(1)
    tb = xnt_ref.shape[1]
    pack = 2 * SUBLANES

    @pl.when(e == 0)
    def _():
        ot_ref[...] = jnp.zeros_like(ot_ref)
        act_ref[...] = jnp.zeros_like(act_ref)
        w_ref[...] = jnp.zeros_like(w_ref)

    cur = e % 2
    prv = 1 - cur
    act_ref[cur] = _dot(u_ref[...], xnt_ref[...])
    ot_ref[...] += _dot(vt_ref[...], w_ref[cur])

    blk = jnp.clip(e - 1, 0, n_blocks - 1)
    zero = jnp.zeros((pack, LANES), BF16)
    for r in range(rows_per_step):
        i0 = blk * rows_per_step + r
        lrows = [lrow_ref[h, pl.ds(i0, 1), :] for h in range(PEER_HEADS)]
        e0rows = [e0_ref[h, pl.ds(i0, 1), :] for h in range(PEER_HEADS)]
        for tc in range(tb // LANES):
            lanes = slice(tc * LANES, (tc + 1) * LANES)
            l_b = [jnp.broadcast_to(lrows[h][:, lanes], (pack, LANES)).astype(BF16)
                   for h in range(PEER_HEADS)]
            e0_b = [jnp.broadcast_to(e0rows[h][:, lanes], (pack, LANES)).astype(BF16)
                    for h in range(PEER_HEADS)]
            for ib in range(N_KEYS // pack):
                wrows = slice(ib * SUBLANES, (ib + 1) * SUBLANES)
                gate = None
                for h in range(PEER_HEADS):
                    rank = pltpu.bitcast(r1_ref[h, wrows, lanes], BF16)
                    e1 = pltpu.bitcast(e1_ref[h, wrows, lanes], BF16)
                    term = jnp.where(rank < l_b[h], e1, zero) * e0_b[h]
                    gate = term if gate is None else gate + term
                rows = slice(r * N_KEYS + ib * pack, r * N_KEYS + (ib + 1) * pack)
                w_ref[prv, rows, lanes] = gate * _gelu_tanh_x2(act_ref[prv, rows, lanes]).astype(BF16)


def _peer_dense(xnt, u16, vt16, lrow, e0, r1, e1, tb, eb):
    d, m = xnt.shape
    n_blocks = u16.shape[0] // eb
    slab_spec = pl.BlockSpec((PEER_HEADS, N_KEYS, tb), lambda i, e: (0, 0, i))
    pack_spec = pl.BlockSpec((PEER_HEADS, N_KEYS // 2, tb), lambda i, e: (0, 0, i))
    return pl.pallas_call(
        functools.partial(_peer_dense_kernel, rows_per_step=eb // N_KEYS, n_blocks=n_blocks),
        grid=(m // tb, n_blocks + 2),
        in_specs=[pl.BlockSpec((d, tb), lambda i, e: (0, i)),
                  pl.BlockSpec((eb, d), lambda i, e: (jnp.minimum(e, n_blocks - 1), 0)),
                  pl.BlockSpec((d, eb), lambda i, e: (0, jnp.clip(e - 2, 0, n_blocks - 1))),
                  slab_spec, slab_spec, pack_spec, pack_spec],
        out_specs=pl.BlockSpec((d, tb), lambda i, e: (0, i)),
        out_shape=jax.ShapeDtypeStruct((d, m), F32),
        scratch_shapes=[pltpu.VMEM((2, eb, tb), F32), pltpu.VMEM((2, eb, tb), BF16)],
        compiler_params=_cparams(("parallel", "arbitrary")),
        name="peer_dense",
    )(xnt, u16, vt16, lrow, e0, r1, e1)


def _ple_kernel(h_ref, po_ref, g_ref, wg_ref, p_ref, wu_ref, fg_ref, o_ref):
    h = h_ref[...] + po_ref[...].T
    xn = _rms(h, g_ref[...]).astype(BF16)
    gate = _sigmoid(_dot(xn, wg_ref[...]))
    up = _dot(p_ref[...].astype(BF16), wu_ref[...])
    o_ref[...] = _rms(h + gate * up, fg_ref[...])


def _ple_final(h, peer_out, g, w_gate, p, w_up, final_g, tm):
    m, d = h.shape
    pd = p.shape[1]
    return pl.pallas_call(
        _ple_kernel,
        grid=(m // tm,),
        in_specs=[pl.BlockSpec((tm, d), lambda i: (i, 0)),
                  pl.BlockSpec((d, tm), lambda i: (0, i)),
                  pl.BlockSpec((1, d), lambda i: (0, 0)),
                  pl.BlockSpec((d, d), lambda i: (0, 0)),
                  pl.BlockSpec((tm, pd), lambda i: (i, 0)),
                  pl.BlockSpec((pd, d), lambda i: (0, 0)),
                  pl.BlockSpec((1, d), lambda i: (0, 0))],
        out_specs=pl.BlockSpec((tm, d), lambda i: (i, 0)),
        out_shape=jax.ShapeDtypeStruct((m, d), F32),
        compiler_params=_cparams(("parallel",)),
        name="ple_final",
    )(h, peer_out, g.reshape(1, d), w_gate.astype(BF16), p, w_up.astype(BF16),
      final_g.reshape(1, d))


def _pick(n, prefs):
    for t in prefs:
        if n % t == 0:
            return t
    return n


def kernel(x, p, mix_norm_g, w_in, rwkv_mu, decay_up, decay_w0, iclr_up, iclr_a0, gate_up, k_k, k_a, r_k, lnx_g, lnx_b, rel_bias, w_out, ffn_norm_g, peer_wq, peer_sub_keys, peer_u, peer_v, ple_norm_g, ple_gate_w, ple_up, final_norm_g):
    batch, seq, d = x.shape
    m = batch * seq
    assert w_in.shape[0] == 1, "the final norm is fused into the (single) layer's last stage"
    rwkv_w = decay_up.shape[2]
    rwkv_cols = rwkv_mu.shape[1]
    h = x.reshape(m, d)
    for i in range(1):
        w16 = w_in[i].astype(BF16)
        tm = _pick(m, (1024, 512, 256, 128))
        z_rwkv = _norm_matmul(h, mix_norm_g[i], w16[:, :rwkv_cols], F32, tm,
                              _pick(rwkv_cols, (1664, 1024, 512, 256, 128)))
        z_att = _norm_matmul(h, mix_norm_g[i], w16[:, rwkv_cols:], BF16, tm,
                             _pick(w16.shape[1] - rwkv_cols, (1536, 1024, 512, 256, 128)))
        y_rwkv = _rwkv(z_rwkv, batch, seq, rwkv_mu[i], decay_up[i], decay_w0[i], iclr_up[i],
                       iclr_a0[i], gate_up[i], k_k[i], k_a[i], r_k[i].reshape(rwkv_w),
                       lnx_g[i], lnx_b[i], _pick(seq, (512, 256, 128, 64)))
        y_att = _attention(z_att, batch, seq, rel_bias[i], _pick(seq, (512,)))
        h1 = _out_proj(h, y_rwkv, y_att, w_out[i], _pick(m, (256, 128)))
        q, xnt = _peer_query(h1, ffn_norm_g[i], peer_wq[i], _pick(m, (256, 128)))
        lrow, e0, r1, e1 = _peer_scores(q, peer_sub_keys[i], _pick(m, (256, 128)))
        peer_out_t = _peer_dense(xnt, peer_u[i].astype(BF16), peer_v[i].T.astype(BF16), lrow, e0, r1,
                                 e1, _pick(m, (512, 256, 128)), 8 * N_KEYS)
        h = _ple_final(h1, peer_out_t, ple_norm_g[i], ple_gate_w[i], p[i].reshape(m, -1), ple_up[i],
                       final_norm_g, _pick(m, (256, 128)))
    return h.reshape(batch, seq, d)
```

```python
import functools

import jax
import jax.numpy as jnp
from jax import lax
from jax.experimental import pallas as pl
from jax.experimental.pallas import tpu as pltpu

F32 = jnp.float32
BF16 = jnp.bfloat16

HEAD_DIM = 64
CHUNK = 64
LEFT_CHUNKS = 8
REL_CLIP = 128
DECAY_LORA = 64
AAA_LORA = 64
GATE_LORA = 128
PEER_HEADS = 8
N_KEYS = 128
PEER_HALF = 128
PEER_TOPK = 16
NORM_EPS = 1e-6
LNX_EPS = 64e-5

VMEM_LIMIT_BYTES = 56 * 1024 * 1024
GROUP_HEADS = 4
GROUP_W = GROUP_HEADS * HEAD_DIM


def _cparams(sem, flags=None):
    return pltpu.CompilerParams(dimension_semantics=sem, vmem_limit_bytes=VMEM_LIMIT_BYTES,
                                flags=flags)


def _dot(a, b):
    return jnp.dot(a, b, preferred_element_type=F32)


def _dot_nt(a, b):
    return lax.dot_general(a, b, (((1,), (1,)), ((), ())), preferred_element_type=F32)


def _dot_tn(a, b):
    return lax.dot_general(a, b, (((0,), (0,)), ((), ())), preferred_element_type=F32)


def _rms(x, g):
    return x * lax.rsqrt(jnp.mean(x * x, axis=-1, keepdims=True) + NORM_EPS) * g


def _sigmoid(x):
    return 1.0 / (1.0 + jnp.exp(-x))


def _norm_matmul_kernel(x_ref, g_ref, w_ref, o_ref, xn_ref):
    @pl.when(pl.program_id(1) == 0)
    def _():
        xn_ref[...] = _rms(x_ref[...], g_ref[...]).astype(BF16)

    o_ref[...] = _dot(xn_ref[...], w_ref[...]).astype(o_ref.dtype)


def _norm_matmul(x, g, w, out_dtype, tm, tn):
    m, d = x.shape
    n = w.shape[1]
    return pl.pallas_call(
        _norm_matmul_kernel,
        grid=(m // tm, n // tn),
        in_specs=[
            pl.BlockSpec((tm, d), lambda i, j: (i, 0)),
            pl.BlockSpec((1, d), lambda i, j: (0, 0)),
            pl.BlockSpec((d, tn), lambda i, j: (0, j)),
        ],
        out_specs=pl.BlockSpec((tm, tn), lambda i, j: (i, j)),
        out_shape=jax.ShapeDtypeStruct((m, n), out_dtype),
        scratch_shapes=[pltpu.VMEM((tm, d), BF16)],
        compiler_params=_cparams(("parallel", "arbitrary")),
        name="norm_matmul",
    )(x, g.reshape(1, d), w)


def _split_dot(x, w_bf16):
    hi = x.astype(BF16)
    lo = (x - hi.astype(F32)).astype(BF16)
    return _dot(hi, w_bf16) + _dot(lo, w_bf16)


def _rwkv_kernel(zr_ref, zk_ref, zv_ref, zl_ref, mur_ref, muk_ref, muv_ref, mul_ref,
                 dup_ref, dw0_ref, iup_ref, ia0_ref, gup_ref, kk_ref, ka_ref, rk_ref,
                 lg_ref, lb_ref, y_ref, state_ref, last_ref, *, tm):
    j = pl.program_id(2)
    n_chunks = tm // CHUNK

    @pl.when(j == 0)
    def _():
        state_ref[...] = jnp.zeros_like(state_ref)
        last_ref[...] = jnp.zeros_like(last_ref)

    row = lax.broadcasted_iota(jnp.int32, (tm, GROUP_W), 0)

    def shift_mix(ref, mu_ref, slot):
        cur = ref[...]
        prev = pltpu.roll(cur, 1, axis=0)
        prev = jnp.where(row == 0, last_ref[slot:slot + 1, :], prev)
        last_ref[slot:slot + 1, :] = cur[tm - 1:tm, :]
        return cur + (prev - cur) * mu_ref[...]

    r = shift_mix(zr_ref, mur_ref, 0)
    k = shift_mix(zk_ref, muk_ref, 1)
    v = shift_mix(zv_ref, muv_ref, 2)
    lo = shift_mix(zl_ref, mul_ref, 3)
    wd = lo[:, :DECAY_LORA]
    ad = lo[:, DECAY_LORA:DECAY_LORA + AAA_LORA]
    gd = lo[:, DECAY_LORA + AAA_LORA:]

    dec = dw0_ref[...] + _dot(jnp.tanh(wd).astype(BF16), dup_ref[...])
    neg = -dec
    softplus = jnp.maximum(neg, 0.0) + jnp.log(1.0 + jnp.exp(-jnp.abs(neg)))
    logw = -jnp.exp(-softplus - 0.5)
    a = _sigmoid(ia0_ref[...] + _dot(ad.astype(BF16), iup_ref[...]))
    g = _dot(_sigmoid(gd).astype(BF16), gup_ref[...])

    hrow = lax.broadcasted_iota(jnp.int32, (GROUP_W, GROUP_W), 0)
    hcol = lax.broadcasted_iota(jnp.int32, (GROUP_W, GROUP_W), 1)
    same = (hrow // HEAD_DIM) == (hcol // HEAD_DIM)
    ones_bd = jnp.where(same, 1.0, 0.0).astype(BF16)

    kkr = k * kk_ref[...]
    kk_norm = jnp.sqrt(_split_dot(kkr * kkr, ones_bd))
    kk = kkr / jnp.maximum(kk_norm, 1e-12)
    k2 = k * (1.0 + (a - 1.0) * ka_ref[...])
    bonus = _split_dot(r * k2 * rk_ref[...], ones_bd) * v
    avec = -kk
    bvec = kk * a

    t_row = hrow % CHUNK
    t_col = hcol % CHUNK
    strict = same & (t_row > t_col)
    incl = same & (t_row >= t_col)
    eye = hrow == hcol
    tri = jnp.where(lax.broadcasted_iota(jnp.int32, (CHUNK, CHUNK), 0)
                    >= lax.broadcasted_iota(jnp.int32, (CHUNK, CHUNK), 1), 1.0, 0.0).astype(BF16)

    def tile4(x):
        return jnp.concatenate([x] * GROUP_HEADS, axis=0)

    def bd16(x):
        return tile4(x.astype(BF16)) * ones_bd

    strict16 = jnp.where(strict, 1.0, 0.0).astype(BF16)
    incl16 = jnp.where(incl, 1.0, 0.0).astype(BF16)
    strict32 = jnp.where(strict, 1.0, 0.0)
    eye32 = jnp.where(eye, 1.0, 0.0)
    chunks = range(n_chunks)
    sls = [slice(c * CHUNK, (c + 1) * CHUNK) for c in chunks]

    cums = []
    for c in chunks:
        lw = logw[sls[c]]
        lw_hi = lw.astype(BF16)
        rem = lw - lw_hi.astype(F32)
        lw_mid = rem.astype(BF16)
        lw_lo = (rem - lw_mid.astype(F32)).astype(BF16)
        cums.append(_dot(tri, lw_hi) + _dot(tri, lw_mid) + _dot(tri, lw_lo))
    bd_a, bd_r, bd_v, bd_be, bd_ke, rhs, p_end, r_hat = [], [], [], [], [], [], [], []
    for c in chunks:
        sl, cum = sls[c], cums[c]
        lw = logw[sl]
        e_cum = jnp.exp(cum)
        e_inv = jnp.exp(-cum)
        e_end = jnp.exp(cum[CHUNK - 1:CHUNK, :] - cum)
        rh = r[sl] * e_cum
        r_hat.append(rh)
        bd_a.append(bd16(avec[sl] * jnp.exp(cum - lw)))
        bd_r.append(bd16(rh))
        bd_v.append(bd16(v[sl]))
        bd_be.append(bd16(bvec[sl] * e_end))
        bd_ke.append(bd16(k2[sl] * e_end))
        rhs.append(jnp.concatenate([tile4((bvec[sl] * e_inv).astype(BF16)),
                                    tile4((k2[sl] * e_inv).astype(BF16))], axis=0))
        p_end.append(e_cum[CHUNK - 1:CHUNK, :])
    lm = [_dot_nt(jnp.concatenate([bd_a[c], bd_r[c]], axis=0), rhs[c]) for c in chunks]
    l_pow = [lm[c][:GROUP_W, :GROUP_W] * strict32 for c in chunks]
    l_ak = [lm[c][:GROUP_W, GROUP_W:].astype(BF16) * strict16 for c in chunks]
    m_rb = [lm[c][GROUP_W:, :GROUP_W].astype(BF16) * incl16 for c in chunks]
    m_rk = [lm[c][GROUP_W:, GROUP_W:].astype(BF16) * incl16 for c in chunks]
    t_inv = [eye32 + l_pow[c] for c in chunks]
    for _ in range(5):
        lp16 = [l_pow[c].astype(BF16) for c in chunks]
        l_pow = [_dot(lp16[c], lp16[c]) for c in chunks]
        t_inv = [t_inv[c] + _dot(t_inv[c].astype(BF16), l_pow[c].astype(BF16)) for c in chunks]
    lakv = [_dot(l_ak[c], bd_v[c]).astype(BF16) for c in chunks]
    tw16 = [_dot(t_inv[c].astype(BF16), jnp.concatenate([bd_a[c], lakv[c]], axis=1)).astype(BF16)
            for c in chunks]
    x1 = [_dot_tn(bd_be[c], tw16[c]) for c in chunks]
    x2 = [_dot_tn(bd_ke[c], bd_v[c]) for c in chunks]
    mt = [_dot(m_rb[c], tw16[c]) for c in chunks]
    mv = [_dot(m_rk[c], bd_v[c]) for c in chunks]
    a_c = [(eye32 * p_end[c] + x1[c][:, :GROUP_W]).astype(BF16) for c in chunks]
    c_c = [x1[c][:, GROUP_W:] + x2[c] for c in chunks]
    q_c = [(jnp.where(same, tile4(r_hat[c]), 0.0) + mt[c][:, :GROUP_W]).astype(BF16)
           for c in chunks]
    y_loc = [mt[c][:, GROUP_W:] + mv[c] for c in chunks]
    ys = []
    state = state_ref[...]
    for c in chunks:
        s16 = state.astype(BF16)
        y_bd = _dot(q_c[c], s16) + y_loc[c]
        state = _dot(a_c[c], s16) + c_c[c]
        y_w = y_bd[0:CHUNK]
        for h in range(1, GROUP_HEADS):
            y_w = y_w + y_bd[h * CHUNK:(h + 1) * CHUNK]
        ys.append(y_w)
    state_ref[...] = state


    y = jnp.concatenate(ys, axis=0)
    mean = _split_dot(y, ones_bd) * (1.0 / HEAD_DIM)
    yc = y - mean
    var = _split_dot(yc * yc, ones_bd) * (1.0 / HEAD_DIM)
    yn = yc * lax.rsqrt(var + LNX_EPS) * lg_ref[...] + lb_ref[...]
    y_ref[...] = ((yn + bonus) * g).astype(y_ref.dtype)


def _rwkv(z, batch, seq, mu, decay_up, decay_w0, iclr_up, iclr_a0, gate_up, k_k, k_a, r_k,
          lnx_g, lnx_b, tm):
    m = z.shape[0]
    width = decay_up.shape[1]
    groups = width // GROUP_W
    nt = seq // tm
    lora_blk = 3 * groups

    def zspec(off):
        return pl.BlockSpec((tm, GROUP_W), lambda b, g, j: (b * nt + j, off(g)))

    def muspec(off):
        return pl.BlockSpec((1, GROUP_W), lambda b, g, j: (0, off(g)))

    def pspec(rows):
        return pl.BlockSpec((rows, GROUP_W), lambda b, g, j: (0, g))

    offs = [lambda g: g, lambda g: groups + g, lambda g: 2 * groups + g, lambda g: lora_blk]
    vec = lambda t: t.reshape(1, width)
    mu2 = mu.reshape(1, -1)
    return pl.pallas_call(
        functools.partial(_rwkv_kernel, tm=tm),
        grid=(batch, groups, nt),
        in_specs=[zspec(o) for o in offs] + [muspec(o) for o in offs] + [
            pspec(DECAY_LORA), pspec(1), pspec(AAA_LORA), pspec(1), pspec(GATE_LORA),
            pspec(1), pspec(1), pspec(1), pspec(1), pspec(1)],
        out_specs=pl.BlockSpec((tm, GROUP_W), lambda b, g, j: (b * nt + j, g)),
        out_shape=jax.ShapeDtypeStruct((m, width), BF16),
        scratch_shapes=[pltpu.VMEM((GROUP_W, GROUP_W), F32), pltpu.VMEM((8, GROUP_W), F32)],
        compiler_params=_cparams(("parallel", "parallel", "arbitrary")),
        name="rwkv7",
    )(z, z, z, z, mu2, mu2, mu2, mu2,
      decay_up.astype(BF16), vec(decay_w0), iclr_up.astype(BF16), vec(iclr_a0),
      gate_up.astype(BF16), vec(k_k), vec(k_a), vec(r_k), vec(lnx_g), vec(lnx_b))


def _attn_kernel(q_ref, kp_ref, kc_ref, vp_ref, vc_ref, bias_ref, o_ref, *, tq):
    j = pl.program_id(2)
    q = q_ref[...]
    kcat = jnp.concatenate([kp_ref[...], kc_ref[...]], axis=0)
    vcat = jnp.concatenate([vp_ref[...], vc_ref[...]], axis=0)
    lane = lax.broadcasted_iota(jnp.int32, (tq, 2 * HEAD_DIM), 1)
    col = lax.broadcasted_iota(jnp.int32, (tq, 2 * tq), 1)
    in_seq = (col >= tq) | (j > 0)
    outs = []
    for hh in range(2):
        mine = (lane // HEAD_DIM) == hh
        qm = jnp.where(mine, q, jnp.zeros_like(q)) * (HEAD_DIM ** -0.5)
        s = _dot_nt(qm.astype(BF16), kcat) + bias_ref[hh]
        s = jnp.where(in_seq, s, -jnp.inf)
        mx = jnp.max(s, axis=-1, keepdims=True)
        p = jnp.exp(s - mx)
        den = jnp.sum(p, axis=-1, keepdims=True)
        outs.append(_dot(p.astype(BF16), vcat) / den)
    o_ref[...] = jnp.where((lane // HEAD_DIM) == 0, outs[0], outs[1]).astype(o_ref.dtype)


def _attention(z_att, batch, seq, rel_bias, tq):
    m, w3 = z_att.shape
    width = w3 // 3
    heads = width // HEAD_DIM
    pairs = heads // 2
    nt = seq // tq
    assert tq == LEFT_CHUNKS * CHUNK
    band_w = (LEFT_CHUNKS + 1) * CHUNK
    qi = jnp.arange(CHUNK)[:, None] + LEFT_CHUNKS * CHUNK
    kj = jnp.arange(band_w)[None, :]
    rel = jnp.clip(qi - kj, -REL_CLIP, REL_CLIP) + REL_CLIP
    onehot = (rel[..., None] == jnp.arange(2 * REL_CLIP + 1)).astype(F32)
    band_bias = jnp.einsum("hr,qkr->hqk", rel_bias.astype(F32), onehot,
                           precision=lax.Precision.HIGHEST)
    bias = jnp.concatenate(
        [jnp.pad(band_bias, ((0, 0), (0, 0), (c * CHUNK, 2 * tq - band_w - c * CHUNK)),
                 constant_values=-jnp.inf) for c in range(tq // CHUNK)], axis=1)

    def spec(col_off, prev):
        if prev:
            return pl.BlockSpec((tq, 2 * HEAD_DIM),
                                lambda b, hp, j: (b * nt + jnp.maximum(j - 1, 0), col_off + hp))
        return pl.BlockSpec((tq, 2 * HEAD_DIM), lambda b, hp, j: (b * nt + j, col_off + hp))

    return pl.pallas_call(
        functools.partial(_attn_kernel, tq=tq),
        grid=(batch, pairs, nt),
        in_specs=[spec(0, False), spec(pairs, True), spec(pairs, False),
                  spec(2 * pairs, True), spec(2 * pairs, False),
                  pl.BlockSpec((2, tq, 2 * tq), lambda b, hp, j: (hp, 0, 0))],
        out_specs=pl.BlockSpec((tq, 2 * HEAD_DIM), lambda b, hp, j: (b * nt + j, hp)),
        out_shape=jax.ShapeDtypeStruct((m, width), BF16),
        compiler_params=_cparams(("parallel", "parallel", "parallel")),
        name="chunk_attention",
    )(z_att, z_att, z_att, z_att, z_att, bias)


def _out_proj_kernel(x_ref, yr_ref, ya_ref, w1_ref, w2_ref, o_ref):
    o_ref[...] = x_ref[...] + _dot(yr_ref[...], w1_ref[...]) + _dot(ya_ref[...], w2_ref[...])


def _out_proj(x, y_rwkv, y_att, w_out, tm):
    m, d = x.shape
    wr = y_rwkv.shape[1]
    wa = y_att.shape[1]
    w16 = w_out.astype(BF16)
    return pl.pallas_call(
        _out_proj_kernel,
        grid=(m // tm,),
        in_specs=[pl.BlockSpec((tm, d), lambda i: (i, 0)),
                  pl.BlockSpec((tm, wr), lambda i: (i, 0)),
                  pl.BlockSpec((tm, wa), lambda i: (i, 0)),
                  pl.BlockSpec((wr, d), lambda i: (0, 0)),
                  pl.BlockSpec((wa, d), lambda i: (0, 0))],
        out_specs=pl.BlockSpec((tm, d), lambda i: (i, 0)),
        out_shape=jax.ShapeDtypeStruct((m, d), F32),
        compiler_params=_cparams(("parallel",)),
        name="out_proj",
    )(x, y_rwkv, y_att, w16[:wr], w16[wr:])


def _peer_query_kernel(h_ref, g_ref, w_ref, q_ref, xnt_ref):
    xn = _rms(h_ref[...], g_ref[...])
    xnt_ref[...] = xn.T.astype(BF16)
    q_ref[...] = _dot(xn.astype(BF16), w_ref[...]).astype(q_ref.dtype)


def _peer_query(h, g, wq, tm):
    m, d = h.shape
    n = wq.shape[1]
    return pl.pallas_call(
        _peer_query_kernel,
        grid=(m // tm,),
        in_specs=[pl.BlockSpec((tm, d), lambda i: (i, 0)),
                  pl.BlockSpec((1, d), lambda i: (0, 0)),
                  pl.BlockSpec((d, n), lambda i: (0, 0))],
        out_specs=[pl.BlockSpec((tm, n), lambda i: (i, 0)),
                   pl.BlockSpec((d, tm), lambda i: (0, i))],
        out_shape=[jax.ShapeDtypeStruct((m, n), BF16), jax.ShapeDtypeStruct((d, m), BF16)],
        compiler_params=_cparams(("parallel",)),
        name="peer_query",
    )(h, g.reshape(1, d), wq.astype(BF16))


NOT_TOP = 127.0


def _top_values(s, count):
    rows = lax.broadcasted_iota(jnp.int32, (count, s.shape[1]), 0)
    tops = jnp.zeros((count, s.shape[1]), F32)
    rank = jnp.full(s.shape, NOT_TOP, F32)
    cur = s
    for i in range(count):
        mx = jnp.max(cur, axis=0, keepdims=True)
        hit = cur == mx
        tops = jnp.where(rows == i, mx, tops)
        rank = jnp.where(hit, float(i), rank)
        cur = jnp.where(hit, -jnp.inf, cur)
    return tops, rank


def _candidate_sums(top0, top1):
    half = PEER_TOPK // 2
    row16 = lax.broadcasted_iota(jnp.int32, top0.shape, 0)
    row8 = lax.broadcasted_iota(jnp.int32, (half, top0.shape[1]), 0)
    ninf = -jnp.inf
    pieces = [top0 + top1[0:1, :],
              jnp.where(row16 >= 1, top0[0:1, :] + top1, ninf)]
    for i in range(1, 4):
        pieces.append(jnp.where(row8 >= i, top0[:half] + top1[i:i + 1, :], ninf))
        if i < 3:
            pieces.append(jnp.where(row8 > i, top0[i:i + 1, :] + top1[:half], ninf))
    return pieces


def _peer_score_kernel(q_ref, keys_ref, lrow_ref, e0_ref, r1_ref, e1_ref):
    for h in range(PEER_HEADS):
        st = []
        for half in range(2):
            c0 = (2 * h + half) * PEER_HALF
            st.append(_dot_nt(keys_ref[2 * h + half], q_ref[:, c0:c0 + PEER_HALF]))
        top0, _ = _top_values(st[0], PEER_TOPK)
        top1, rank1 = _top_values(st[1], PEER_TOPK)
        cands = _candidate_sums(top0, top1)
        cur = list(cands)
        thr = None
        for _ in range(PEER_TOPK):
            big = jnp.maximum(cur[0], cur[1])
            small = cur[2]
            for c in cur[3:]:
                small = jnp.maximum(small, c)
            thr = jnp.maximum(jnp.max(big, axis=0, keepdims=True), jnp.max(small, axis=0, keepdims=True))
            cur = [jnp.where(c == thr, -jnp.inf, c) for c in cur]
        best = top0[0:1, :] + top1[0:1, :]
        den = jnp.zeros_like(thr)
        for c in cands:
            den = den + jnp.sum(jnp.where(c >= thr, jnp.exp(c - best), 0.0), axis=0, keepdims=True)
        lrow = jnp.zeros_like(st[0])
        for b in range(PEER_TOPK):
            lrow = lrow + jnp.where(st[0] + top1[b:b + 1, :] >= thr, 1.0, 0.0)
        lrow_ref[h] = lrow
        e0_ref[h] = jnp.exp(st[0] - top0[0:1, :]) * (0.5 / den)
        r1_ref[h] = pltpu.bitcast(rank1.astype(BF16), jnp.uint32)
        e1_ref[h] = pltpu.bitcast(jnp.exp(st[1] - top1[0:1, :]).astype(BF16), jnp.uint32)


def _peer_scores(q, sub_keys, tb):
    m = q.shape[0]
    keys = sub_keys.reshape(2 * PEER_HEADS, N_KEYS, PEER_HALF).astype(BF16)
    slab_spec = pl.BlockSpec((PEER_HEADS, N_KEYS, tb), lambda i: (0, 0, i))
    slab = lambda dt: jax.ShapeDtypeStruct((PEER_HEADS, N_KEYS, m), dt)
    pack_spec = pl.BlockSpec((PEER_HEADS, N_KEYS // 2, tb), lambda i: (0, 0, i))
    packed = jax.ShapeDtypeStruct((PEER_HEADS, N_KEYS // 2, m), jnp.uint32)
    return pl.pallas_call(
        _peer_score_kernel,
        grid=(m // tb,),
        in_specs=[pl.BlockSpec((tb, q.shape[1]), lambda i: (i, 0)),
                  pl.BlockSpec(keys.shape, lambda i: (0, 0, 0))],
        out_specs=[slab_spec, slab_spec, pack_spec, pack_spec],
        out_shape=[slab(F32), slab(F32), packed, packed],
        compiler_params=_cparams(("parallel",)),
        name="peer_scores",
    )(q, keys)


def _gelu_tanh_x2(x):
    c = 0.7978845608028654
    return x * (1.0 + jnp.tanh(x * (c + (c * 0.044715) * (x * x))))


LANES = 128
SUBLANES = 8


def _peer_dense_kernel(xnt_ref, u_ref, vt_ref, lrow_ref, e0_ref, r1_ref, e1_ref,
                       ot_ref, act0_ref, act1_ref, w_ref, *, rows_per_step, n_blocks):
    e = pl.program_id(1)
    tb = xnt_ref.shape[1]
    pack = 2 * SUBLANES

    @pl.when(e == 0)
    def _():
        ot_ref[...] = jnp.zeros_like(ot_ref)
        act0_ref[...] = jnp.zeros_like(act0_ref)
        act1_ref[...] = jnp.zeros_like(act1_ref)

    blk = jnp.clip(e - 1, 0, n_blocks - 1)
    zero = jnp.zeros((pack, LANES), BF16)

    def step(act_out, act_in):
        for r in range(rows_per_step):
            i0 = blk * rows_per_step + r
            lrows = [lrow_ref[h, pl.ds(i0, 1), :] for h in range(PEER_HEADS)]
            e0rows = [e0_ref[h, pl.ds(i0, 1), :] for h in range(PEER_HEADS)]
            for tc in range(tb // LANES):
                lanes = slice(tc * LANES, (tc + 1) * LANES)
                l_b = [jnp.broadcast_to(lrows[h][:, lanes], (pack, LANES)).astype(BF16)
                       for h in range(PEER_HEADS)]
                e0_b = [jnp.broadcast_to(e0rows[h][:, lanes], (pack, LANES)).astype(BF16)
                        for h in range(PEER_HEADS)]
                for ib in range(N_KEYS // pack):
                    wrows = slice(ib * SUBLANES, (ib + 1) * SUBLANES)
                    gate = None
                    for h in range(PEER_HEADS):
                        rank = pltpu.bitcast(r1_ref[h, wrows, lanes], BF16)
                        e1 = pltpu.bitcast(e1_ref[h, wrows, lanes], BF16)
                        term = jnp.where(rank < l_b[h], e1, zero) * e0_b[h]
                        gate = term if gate is None else gate + term
                    rows = slice(r * N_KEYS + ib * pack, r * N_KEYS + (ib + 1) * pack)
                    w_ref[rows, lanes] = gate * _gelu_tanh_x2(act_in[rows, lanes]).astype(BF16)
        ot_ref[...] += _dot(vt_ref[...], w_ref[...])
        act_out[...] = _dot(u_ref[...], xnt_ref[...])

    @pl.when(e % 2 == 0)
    def _():
        step(act0_ref, act1_ref)

    @pl.when(e % 2 == 1)
    def _():
        step(act1_ref, act0_ref)


def _peer_dense(xnt, u16, vt16, lrow, e0, r1, e1, tb, eb):
    d, m = xnt.shape
    n_blocks = u16.shape[0] // eb
    slab_spec = pl.BlockSpec((PEER_HEADS, N_KEYS, tb), lambda i, e: (0, 0, i))
    pack_spec = pl.BlockSpec((PEER_HEADS, N_KEYS // 2, tb), lambda i, e: (0, 0, i))
    return pl.pallas_call(
        functools.partial(_peer_dense_kernel, rows_per_step=eb // N_KEYS, n_blocks=n_blocks),
        grid=(m // tb, n_blocks + 1),
        in_specs=[pl.BlockSpec((d, tb), lambda i, e: (0, i)),
                  pl.BlockSpec((eb, d), lambda i, e: (jnp.minimum(e, n_blocks - 1), 0)),
                  pl.BlockSpec((d, eb), lambda i, e: (0, jnp.clip(e - 1, 0, n_blocks - 1))),
                  slab_spec, slab_spec, pack_spec, pack_spec],
        out_specs=pl.BlockSpec((d, tb), lambda i, e: (0, i)),
        out_shape=jax.ShapeDtypeStruct((d, m), F32),
        scratch_shapes=[pltpu.VMEM((eb, tb), F32), pltpu.VMEM((eb, tb), F32), pltpu.VMEM((eb, tb), BF16)],
        compiler_params=_cparams(("parallel", "arbitrary")),
        name="peer_dense",
    )(xnt, u16, vt16, lrow, e0, r1, e1)


def _ple_kernel(h_ref, po_ref, g_ref, wg_ref, p_ref, wu_ref, fg_ref, o_ref):
    h = h_ref[...] + po_ref[...].T
    xn = _rms(h, g_ref[...]).astype(BF16)
    gate = _sigmoid(_dot(xn, wg_ref[...]))
    up = _dot(p_ref[...].astype(BF16), wu_ref[...])
    o_ref[...] = _rms(h + gate * up, fg_ref[...])


def _ple_final(h, peer_out, g, w_gate, p, w_up, final_g, tm):
    m, d = h.shape
    pd = p.shape[1]
    return pl.pallas_call(
        _ple_kernel,
        grid=(m // tm,),
        in_specs=[pl.BlockSpec((tm, d), lambda i: (i, 0)),
                  pl.BlockSpec((d, tm), lambda i: (0, i)),
                  pl.BlockSpec((1, d), lambda i: (0, 0)),
                  pl.BlockSpec((d, d), lambda i: (0, 0)),
                  pl.BlockSpec((tm, pd), lambda i: (i, 0)),
                  pl.BlockSpec((pd, d), lambda i: (0, 0)),
                  pl.BlockSpec((1, d), lambda i: (0, 0))],
        out_specs=pl.BlockSpec((tm, d), lambda i: (i, 0)),
        out_shape=jax.ShapeDtypeStruct((m, d), F32),
        compiler_params=_cparams(("parallel",)),
        name="ple_final",
    )(h, peer_out, g.reshape(1, d), w_gate.astype(BF16), p, w_up.astype(BF16),
      final_g.reshape(1, d))


def _pick(n, prefs):
    for t in prefs:
        if n % t == 0:
            return t
    return n


def kernel(x, p, mix_norm_g, w_in, rwkv_mu, decay_up, decay_w0, iclr_up, iclr_a0, gate_up, k_k, k_a, r_k, lnx_g, lnx_b, rel_bias, w_out, ffn_norm_g, peer_wq, peer_sub_keys, peer_u, peer_v, ple_norm_g, ple_gate_w, ple_up, final_norm_g):
    batch, seq, d = x.shape
    m = batch * seq
    assert w_in.shape[0] == 1, "the final norm is fused into the (single) layer's last stage"
    rwkv_w = decay_up.shape[2]
    rwkv_cols = rwkv_mu.shape[1]
    h = x.reshape(m, d)
    for i in range(1):
        w16 = w_in[i].astype(BF16)
        tm = _pick(m, (1024, 512, 256, 128))
        z_rwkv = _norm_matmul(h, mix_norm_g[i], w16[:, :rwkv_cols], F32, tm,
                              _pick(rwkv_cols, (1664, 1024, 512, 256, 128)))
        z_att = _norm_matmul(h, mix_norm_g[i], w16[:, rwkv_cols:], BF16, tm,
                             _pick(w16.shape[1] - rwkv_cols, (1536, 1024, 512, 256, 128)))
        y_rwkv = _rwkv(z_rwkv, batch, seq, rwkv_mu[i], decay_up[i], decay_w0[i], iclr_up[i],
                       iclr_a0[i], gate_up[i], k_k[i], k_a[i], r_k[i].reshape(rwkv_w),
                       lnx_g[i], lnx_b[i], _pick(seq, (512, 256, 128, 64)))
        y_att = _attention(z_att, batch, seq, rel_bias[i], _pick(seq, (512,)))
        h1 = _out_proj(h, y_rwkv, y_att, w_out[i], _pick(m, (256, 128)))
        q, xnt = _peer_query(h1, ffn_norm_g[i], peer_wq[i], _pick(m, (256, 128)))
        lrow, e0, r1, e1 = _peer_scores(q, peer_sub_keys[i], _pick(m, (256, 128)))
        peer_out_t = _peer_dense(xnt, peer_u[i].astype(BF16), peer_v[i].T.astype(BF16), lrow, e0, r1,
                                 e1, _pick(m, (512, 256, 128)), 8 * N_KEYS)
        h = _ple_final(h1, peer_out_t, ple_norm_g[i], ple_gate_w[i], p[i].reshape(m, -1), ple_up[i],
                       final_norm_g, _pick(m, (256, 128)))
    return h.reshape(batch, seq, d)
```

```python
import functools

import jax
import jax.numpy as jnp
from jax import lax
from jax.experimental import pallas as pl
from jax.experimental.pallas import tpu as pltpu

F32 = jnp.float32
BF16 = jnp.bfloat16

HEAD_DIM = 64
CHUNK = 64
LEFT_CHUNKS = 8
REL_CLIP = 128
DECAY_LORA = 64
AAA_LORA = 64
GATE_LORA = 128
PEER_HEADS = 8
N_KEYS = 128
PEER_HALF = 128
PEER_TOPK = 16
NORM_EPS = 1e-6
LNX_EPS = 64e-5

VMEM_LIMIT_BYTES = 56 * 1024 * 1024
GROUP_HEADS = 4
GROUP_W = GROUP_HEADS * HEAD_DIM
LANES = 128
SUBLANES = 8


def _cparams(sem):
    return pltpu.CompilerParams(dimension_semantics=sem, vmem_limit_bytes=VMEM_LIMIT_BYTES)


def _dot(a, b):
    return jnp.dot(a, b, preferred_element_type=F32)


def _dot_nt(a, b):
    return lax.dot_general(a, b, (((1,), (1,)), ((), ())), preferred_element_type=F32)


def _dot_tn(a, b):
    return lax.dot_general(a, b, (((0,), (0,)), ((), ())), preferred_element_type=F32)


def _rms(x, g):
    return x * lax.rsqrt(jnp.mean(x * x, axis=-1, keepdims=True) + NORM_EPS) * g


def _sigmoid(x):
    return 1.0 / (1.0 + jnp.exp(-x))


def _norm_matmul_kernel(x_ref, g_ref, w_ref, o_ref, xn_ref):
    @pl.when(pl.program_id(1) == 0)
    def _():
        xn_ref[...] = _rms(x_ref[...], g_ref[...]).astype(BF16)

    o_ref[...] = _dot(xn_ref[...], w_ref[...]).astype(o_ref.dtype)


def _norm_matmul(x, g, w, out_dtype, tm, tn):
    m, d = x.shape
    n = w.shape[1]
    return pl.pallas_call(
        _norm_matmul_kernel,
        grid=(m // tm, n // tn),
        in_specs=[
            pl.BlockSpec((tm, d), lambda i, j: (i, 0)),
            pl.BlockSpec((1, d), lambda i, j: (0, 0)),
            pl.BlockSpec((d, tn), lambda i, j: (0, j)),
        ],
        out_specs=pl.BlockSpec((tm, tn), lambda i, j: (i, j)),
        out_shape=jax.ShapeDtypeStruct((m, n), out_dtype),
        scratch_shapes=[pltpu.VMEM((tm, d), BF16)],
        compiler_params=_cparams(("parallel", "arbitrary")),
        name="norm_matmul",
    )(x, g.reshape(1, d), w)


def _split_dot(x, w_bf16):
    hi = x.astype(BF16)
    lo = (x - hi.astype(F32)).astype(BF16)
    return _dot(hi, w_bf16) + _dot(lo, w_bf16)


def _rwkv_kernel(zr_ref, zk_ref, zv_ref, zl_ref, mur_ref, muk_ref, muv_ref, mul_ref,
                 dup_ref, dw0_ref, iup_ref, ia0_ref, gup_ref, kk_ref, ka_ref, rk_ref,
                 lg_ref, lb_ref, y_ref, state_ref, last_ref, *, tm):
    j = pl.program_id(2)
    n_chunks = tm // CHUNK

    @pl.when(j == 0)
    def _():
        state_ref[...] = jnp.zeros_like(state_ref)
        last_ref[...] = jnp.zeros_like(last_ref)

    row = lax.broadcasted_iota(jnp.int32, (tm, GROUP_W), 0)

    def shift_mix(ref, mu_ref, slot):
        cur = ref[...]
        prev = pltpu.roll(cur, 1, axis=0)
        prev = jnp.where(row == 0, last_ref[slot:slot + 1, :], prev)
        last_ref[slot:slot + 1, :] = cur[tm - 1:tm, :]
        return cur + (prev - cur) * mu_ref[...]

    r = shift_mix(zr_ref, mur_ref, 0)
    k = shift_mix(zk_ref, muk_ref, 1)
    v = shift_mix(zv_ref, muv_ref, 2)
    lo = shift_mix(zl_ref, mul_ref, 3)
    wd = lo[:, :DECAY_LORA]
    ad = lo[:, DECAY_LORA:DECAY_LORA + AAA_LORA]
    gd = lo[:, DECAY_LORA + AAA_LORA:]

    dec = dw0_ref[...] + _dot(jnp.tanh(wd).astype(BF16), dup_ref[...])
    neg = -dec
    softplus = jnp.maximum(neg, 0.0) + jnp.log(1.0 + jnp.exp(-jnp.abs(neg)))
    logw = -jnp.exp(-softplus - 0.5)
    a = _sigmoid(ia0_ref[...] + _dot(ad.astype(BF16), iup_ref[...]))
    g = _dot(_sigmoid(gd).astype(BF16), gup_ref[...])

    hrow = lax.broadcasted_iota(jnp.int32, (GROUP_W, GROUP_W), 0)
    hcol = lax.broadcasted_iota(jnp.int32, (GROUP_W, GROUP_W), 1)
    same = (hrow // HEAD_DIM) == (hcol // HEAD_DIM)
    ones_bd = jnp.where(same, 1.0, 0.0).astype(BF16)

    kkr = k * kk_ref[...]
    kk_norm = jnp.sqrt(_split_dot(kkr * kkr, ones_bd))
    kk = kkr / jnp.maximum(kk_norm, 1e-12)
    k2 = k * (1.0 + (a - 1.0) * ka_ref[...])
    bonus = _split_dot(r * k2 * rk_ref[...], ones_bd) * v
    avec = -kk
    bvec = kk * a

    t_row = hrow % CHUNK
    t_col = hcol % CHUNK
    strict = same & (t_row > t_col)
    incl = same & (t_row >= t_col)
    eye = hrow == hcol
    tri = jnp.where(lax.broadcasted_iota(jnp.int32, (CHUNK, CHUNK), 0)
                    >= lax.broadcasted_iota(jnp.int32, (CHUNK, CHUNK), 1), 1.0, 0.0).astype(BF16)

    def tile4(x):
        return jnp.concatenate([x] * GROUP_HEADS, axis=0)

    def bd16(x):
        return tile4(x.astype(BF16)) * ones_bd

    strict16 = jnp.where(strict, 1.0, 0.0).astype(BF16)
    incl16 = jnp.where(incl, 1.0, 0.0).astype(BF16)
    strict32 = jnp.where(strict, 1.0, 0.0)
    eye32 = jnp.where(eye, 1.0, 0.0)
    chunks = range(n_chunks)
    sls = [slice(c * CHUNK, (c + 1) * CHUNK) for c in chunks]

    cums = []
    for c in chunks:
        lw = logw[sls[c]]
        lw_hi = lw.astype(BF16)
        rem = lw - lw_hi.astype(F32)
        lw_mid = rem.astype(BF16)
        lw_lo = (rem - lw_mid.astype(F32)).astype(BF16)
        cums.append(_dot(tri, lw_hi) + _dot(tri, lw_mid) + _dot(tri, lw_lo))
    bd_a, bd_r, bd_v, bd_be, bd_ke, rhs, p_end, r_hat = [], [], [], [], [], [], [], []
    for c in chunks:
        sl, cum = sls[c], cums[c]
        lw = logw[sl]
        e_cum = jnp.exp(cum)
        e_inv = jnp.exp(-cum)
        e_end = jnp.exp(cum[CHUNK - 1:CHUNK, :] - cum)
        rh = r[sl] * e_cum
        r_hat.append(rh)
        bd_a.append(bd16(avec[sl] * jnp.exp(cum - lw)))
        bd_r.append(bd16(rh))
        bd_v.append(bd16(v[sl]))
        bd_be.append(bd16(bvec[sl] * e_end))
        bd_ke.append(bd16(k2[sl] * e_end))
        rhs.append(jnp.concatenate([tile4((bvec[sl] * e_inv).astype(BF16)),
                                    tile4((k2[sl] * e_inv).astype(BF16))], axis=0))
        p_end.append(e_cum[CHUNK - 1:CHUNK, :])
    lm = [_dot_nt(jnp.concatenate([bd_a[c], bd_r[c]], axis=0), rhs[c]) for c in chunks]
    l_pow = [lm[c][:GROUP_W, :GROUP_W] * strict32 for c in chunks]
    l_ak = [lm[c][:GROUP_W, GROUP_W:].astype(BF16) * strict16 for c in chunks]
    m_rb = [lm[c][GROUP_W:, :GROUP_W].astype(BF16) * incl16 for c in chunks]
    m_rk = [lm[c][GROUP_W:, GROUP_W:].astype(BF16) * incl16 for c in chunks]
    t_inv = [eye32 + l_pow[c] for c in chunks]
    for _ in range(5):
        lp16 = [l_pow[c].astype(BF16) for c in chunks]
        l_pow = [_dot(lp16[c], lp16[c]) for c in chunks]
        t_inv = [t_inv[c] + _dot(t_inv[c].astype(BF16), l_pow[c].astype(BF16)) for c in chunks]
    lakv = [_dot(l_ak[c], bd_v[c]).astype(BF16) for c in chunks]
    tw16 = [_dot(t_inv[c].astype(BF16), jnp.concatenate([bd_a[c], lakv[c]], axis=1)).astype(BF16)
            for c in chunks]
    x1 = [_dot_tn(bd_be[c], tw16[c]) for c in chunks]
    x2 = [_dot_tn(bd_ke[c], bd_v[c]) for c in chunks]
    mt = [_dot(m_rb[c], tw16[c]) for c in chunks]
    mv = [_dot(m_rk[c], bd_v[c]) for c in chunks]
    a_c = [(eye32 * p_end[c] + x1[c][:, :GROUP_W]).astype(BF16) for c in chunks]
    c_c = [x1[c][:, GROUP_W:] + x2[c] for c in chunks]
    q_c = [(jnp.where(same, tile4(r_hat[c]), 0.0) + mt[c][:, :GROUP_W]).astype(BF16)
           for c in chunks]
    y_loc = [mt[c][:, GROUP_W:] + mv[c] for c in chunks]
    ys = []
    state = state_ref[...]
    for c in chunks:
        s16 = state.astype(BF16)
        y_bd = _dot(q_c[c], s16) + y_loc[c]
        state = _dot(a_c[c], s16) + c_c[c]
        y_w = y_bd[0:CHUNK]
        for h in range(1, GROUP_HEADS):
            y_w = y_w + y_bd[h * CHUNK:(h + 1) * CHUNK]
        ys.append(y_w)
    state_ref[...] = state


    y = jnp.concatenate(ys, axis=0)
    mean = _split_dot(y, ones_bd) * (1.0 / HEAD_DIM)
    yc = y - mean
    var = _split_dot(yc * yc, ones_bd) * (1.0 / HEAD_DIM)
    yn = yc * lax.rsqrt(var + LNX_EPS) * lg_ref[...] + lb_ref[...]
    y_ref[...] = ((yn + bonus) * g).astype(y_ref.dtype)


def _rwkv(z, batch, seq, mu, decay_up, decay_w0, iclr_up, iclr_a0, gate_up, k_k, k_a, r_k,
          lnx_g, lnx_b, tm):
    m = z.shape[0]
    width = decay_up.shape[1]
    groups = width // GROUP_W
    nt = seq // tm
    lora_blk = 3 * groups

    def zspec(off):
        return pl.BlockSpec((tm, GROUP_W), lambda b, g, j: (b * nt + j, off(g)))

    def muspec(off):
        return pl.BlockSpec((1, GROUP_W), lambda b, g, j: (0, off(g)))

    def pspec(rows):
        return pl.BlockSpec((rows, GROUP_W), lambda b, g, j: (0, g))

    offs = [lambda g: g, lambda g: groups + g, lambda g: 2 * groups + g, lambda g: lora_blk]
    vec = lambda t: t.reshape(1, width)
    mu2 = mu.reshape(1, -1)
    return pl.pallas_call(
        functools.partial(_rwkv_kernel, tm=tm),
        grid=(batch, groups, nt),
        in_specs=[zspec(o) for o in offs] + [muspec(o) for o in offs] + [
            pspec(DECAY_LORA), pspec(1), pspec(AAA_LORA), pspec(1), pspec(GATE_LORA),
            pspec(1), pspec(1), pspec(1), pspec(1), pspec(1)],
        out_specs=pl.BlockSpec((tm, GROUP_W), lambda b, g, j: (b * nt + j, g)),
        out_shape=jax.ShapeDtypeStruct((m, width), BF16),
        scratch_shapes=[pltpu.VMEM((GROUP_W, GROUP_W), F32), pltpu.VMEM((8, GROUP_W), F32)],
        compiler_params=_cparams(("parallel", "parallel", "arbitrary")),
        name="rwkv7",
    )(z, z, z, z, mu2, mu2, mu2, mu2,
      decay_up.astype(BF16), vec(decay_w0), iclr_up.astype(BF16), vec(iclr_a0),
      gate_up.astype(BF16), vec(k_k), vec(k_a), vec(r_k), vec(lnx_g), vec(lnx_b))


def _attn_kernel(q_ref, kp_ref, kc_ref, vp_ref, vc_ref, bias_ref, o_ref, *, tq):
    j = pl.program_id(2)
    q = q_ref[...]
    kcat = jnp.concatenate([kp_ref[...], kc_ref[...]], axis=0)
    vcat = jnp.concatenate([vp_ref[...], vc_ref[...]], axis=0)
    lane = lax.broadcasted_iota(jnp.int32, (tq, 2 * HEAD_DIM), 1)
    col = lax.broadcasted_iota(jnp.int32, (tq, 2 * tq), 1)
    in_seq = (col >= tq) | (j > 0)
    outs = []
    for hh in range(2):
        mine = (lane // HEAD_DIM) == hh
        qm = jnp.where(mine, q, jnp.zeros_like(q)) * (HEAD_DIM ** -0.5)
        s = _dot_nt(qm.astype(BF16), kcat) + bias_ref[hh]
        s = jnp.where(in_seq, s, -jnp.inf)
        mx = jnp.max(s, axis=-1, keepdims=True)
        p = jnp.exp(s - mx)
        den = jnp.sum(p, axis=-1, keepdims=True)
        outs.append(_dot(p.astype(BF16), vcat) / den)
    o_ref[...] = jnp.where((lane // HEAD_DIM) == 0, outs[0], outs[1]).astype(o_ref.dtype)


def _attention(z_att, batch, seq, rel_bias, tq):
    m, w3 = z_att.shape
    width = w3 // 3
    heads = width // HEAD_DIM
    pairs = heads // 2
    nt = seq // tq
    assert tq == LEFT_CHUNKS * CHUNK
    band_w = (LEFT_CHUNKS + 1) * CHUNK
    qi = jnp.arange(CHUNK)[:, None] + LEFT_CHUNKS * CHUNK
    kj = jnp.arange(band_w)[None, :]
    rel = jnp.clip(qi - kj, -REL_CLIP, REL_CLIP) + REL_CLIP
    onehot = (rel[..., None] == jnp.arange(2 * REL_CLIP + 1)).astype(F32)
    band_bias = jnp.einsum("hr,qkr->hqk", rel_bias.astype(F32), onehot,
                           precision=lax.Precision.HIGHEST)
    bias = jnp.concatenate(
        [jnp.pad(band_bias, ((0, 0), (0, 0), (c * CHUNK, 2 * tq - band_w - c * CHUNK)),
                 constant_values=-jnp.inf) for c in range(tq // CHUNK)], axis=1)

    def spec(col_off, prev):
        if prev:
            return pl.BlockSpec((tq, 2 * HEAD_DIM),
                                lambda b, hp, j: (b * nt + jnp.maximum(j - 1, 0), col_off + hp))
        return pl.BlockSpec((tq, 2 * HEAD_DIM), lambda b, hp, j: (b * nt + j, col_off + hp))

    return pl.pallas_call(
        functools.partial(_attn_kernel, tq=tq),
        grid=(batch, pairs, nt),
        in_specs=[spec(0, False), spec(pairs, True), spec(pairs, False),
                  spec(2 * pairs, True), spec(2 * pairs, False),
                  pl.BlockSpec((2, tq, 2 * tq), lambda b, hp, j: (hp, 0, 0))],
        out_specs=pl.BlockSpec((tq, 2 * HEAD_DIM), lambda b, hp, j: (b * nt + j, hp)),
        out_shape=jax.ShapeDtypeStruct((m, width), BF16),
        compiler_params=_cparams(("parallel", "parallel", "parallel")),
        name="chunk_attention",
    )(z_att, z_att, z_att, z_att, z_att, bias)


def _out_proj_kernel(x_ref, yr_ref, ya_ref, w1_ref, w2_ref, o_ref):
    o_ref[...] = x_ref[...] + _dot(yr_ref[...], w1_ref[...]) + _dot(ya_ref[...], w2_ref[...])


def _out_proj(x, y_rwkv, y_att, w_out, tm):
    m, d = x.shape
    wr = y_rwkv.shape[1]
    wa = y_att.shape[1]
    w16 = w_out.astype(BF16)
    return pl.pallas_call(
        _out_proj_kernel,
        grid=(m // tm,),
        in_specs=[pl.BlockSpec((tm, d), lambda i: (i, 0)),
                  pl.BlockSpec((tm, wr), lambda i: (i, 0)),
                  pl.BlockSpec((tm, wa), lambda i: (i, 0)),
                  pl.BlockSpec((wr, d), lambda i: (0, 0)),
                  pl.BlockSpec((wa, d), lambda i: (0, 0))],
        out_specs=pl.BlockSpec((tm, d), lambda i: (i, 0)),
        out_shape=jax.ShapeDtypeStruct((m, d), F32),
        compiler_params=_cparams(("parallel",)),
        name="out_proj",
    )(x, y_rwkv, y_att, w16[:wr], w16[wr:])


def _peer_query_kernel(h_ref, g_ref, w_ref, q_ref, xnt_ref):
    xn = _rms(h_ref[...], g_ref[...])
    xnt_ref[...] = xn.T.astype(BF16)
    q_ref[...] = _dot(xn.astype(BF16), w_ref[...]).astype(q_ref.dtype)


def _peer_query(h, g, wq, tm):
    m, d = h.shape
    n = wq.shape[1]
    return pl.pallas_call(
        _peer_query_kernel,
        grid=(m // tm,),
        in_specs=[pl.BlockSpec((tm, d), lambda i: (i, 0)),
                  pl.BlockSpec((1, d), lambda i: (0, 0)),
                  pl.BlockSpec((d, n), lambda i: (0, 0))],
        out_specs=[pl.BlockSpec((tm, n), lambda i: (i, 0)),
                   pl.BlockSpec((d, tm), lambda i: (0, i))],
        out_shape=[jax.ShapeDtypeStruct((m, n), BF16), jax.ShapeDtypeStruct((d, m), BF16)],
        compiler_params=_cparams(("parallel",)),
        name="peer_query",
    )(h, g.reshape(1, d), wq.astype(BF16))


NOT_TOP = 127.0


def _top_values(s, count):
    rows = lax.broadcasted_iota(jnp.int32, (count, s.shape[1]), 0)
    tops = jnp.zeros((count, s.shape[1]), F32)
    rank = jnp.full(s.shape, NOT_TOP, F32)
    cur = s
    for i in range(count):
        mx = jnp.max(cur, axis=0, keepdims=True)
        hit = cur == mx
        tops = jnp.where(rows == i, mx, tops)
        rank = jnp.where(hit, float(i), rank)
        cur = jnp.where(hit, -jnp.inf, cur)
    return tops, rank


def _candidate_sums(top0, top1):
    half = PEER_TOPK // 2
    row16 = lax.broadcasted_iota(jnp.int32, top0.shape, 0)
    row8 = lax.broadcasted_iota(jnp.int32, (half, top0.shape[1]), 0)
    ninf = -jnp.inf
    pieces = [top0 + top1[0:1, :],
              jnp.where(row16 >= 1, top0[0:1, :] + top1, ninf)]
    for i in range(1, 4):
        pieces.append(jnp.where(row8 >= i, top0[:half] + top1[i:i + 1, :], ninf))
        if i < 3:
            pieces.append(jnp.where(row8 > i, top0[i:i + 1, :] + top1[:half], ninf))
    return pieces


def _peer_score_kernel(q_ref, keys_ref, lrow_ref, e0_ref, r1_ref, e1_ref):
    for h in range(PEER_HEADS):
        st = []
        for half in range(2):
            c0 = (2 * h + half) * PEER_HALF
            st.append(_dot_nt(keys_ref[2 * h + half], q_ref[:, c0:c0 + PEER_HALF]))
        top0, _ = _top_values(st[0], PEER_TOPK)
        top1, rank1 = _top_values(st[1], PEER_TOPK)
        cands = _candidate_sums(top0, top1)
        cur = list(cands)
        thr = None
        for _ in range(PEER_TOPK):
            big = jnp.maximum(cur[0], cur[1])
            small = cur[2]
            for c in cur[3:]:
                small = jnp.maximum(small, c)
            thr = jnp.maximum(jnp.max(big, axis=0, keepdims=True), jnp.max(small, axis=0, keepdims=True))
            cur = [jnp.where(c == thr, -jnp.inf, c) for c in cur]
        best = top0[0:1, :] + top1[0:1, :]
        den = jnp.zeros_like(thr)
        for c in cands:
            den = den + jnp.sum(jnp.where(c >= thr, jnp.exp(c - best), 0.0), axis=0, keepdims=True)
        lrow = jnp.zeros_like(st[0])
        for b in range(PEER_TOPK):
            lrow = lrow + jnp.where(st[0] + top1[b:b + 1, :] >= thr, 1.0, 0.0)
        lrow_ref[h] = lrow
        e0_ref[h] = jnp.exp(st[0] - top0[0:1, :]) * (0.5 / den)
        r1_ref[h] = pltpu.bitcast(rank1.astype(BF16), jnp.uint32)
        e1_ref[h] = pltpu.bitcast(jnp.exp(st[1] - top1[0:1, :]).astype(BF16), jnp.uint32)


def _peer_scores(q, sub_keys, tb):
    m = q.shape[0]
    keys = sub_keys.reshape(2 * PEER_HEADS, N_KEYS, PEER_HALF).astype(BF16)
    slab_spec = pl.BlockSpec((PEER_HEADS, N_KEYS, tb), lambda i: (0, 0, i))
    slab = lambda dt: jax.ShapeDtypeStruct((PEER_HEADS, N_KEYS, m), dt)
    pack_spec = pl.BlockSpec((PEER_HEADS, N_KEYS // 2, tb), lambda i: (0, 0, i))
    packed = jax.ShapeDtypeStruct((PEER_HEADS, N_KEYS // 2, m), jnp.uint32)
    return pl.pallas_call(
        _peer_score_kernel,
        grid=(m // tb,),
        in_specs=[pl.BlockSpec((tb, q.shape[1]), lambda i: (i, 0)),
                  pl.BlockSpec(keys.shape, lambda i: (0, 0, 0))],
        out_specs=[slab_spec, slab_spec, pack_spec, pack_spec],
        out_shape=[slab(F32), slab(F32), packed, packed],
        compiler_params=_cparams(("parallel",)),
        name="peer_scores",
    )(q, keys)


def _gelu_tanh_x2(x):
    c = 0.7978845608028654
    return x * (1.0 + jnp.tanh(x * (c + (c * 0.044715) * (x * x))))


def _peer_dense_kernel(xnt_ref, u_ref, vt_ref, lrow_ref, e0_ref, r1_ref, e1_ref,
                       ot_ref, act0_ref, act1_ref, w_ref, *, rows_per_step, n_blocks):
    e = pl.program_id(1)
    tb = xnt_ref.shape[1]
    pack = 2 * SUBLANES

    @pl.when(e == 0)
    def _():
        ot_ref[...] = jnp.zeros_like(ot_ref)
        act0_ref[...] = jnp.zeros_like(act0_ref)
        act1_ref[...] = jnp.zeros_like(act1_ref)

    blk = jnp.clip(e - 1, 0, n_blocks - 1)
    zero = jnp.zeros((pack, LANES), BF16)

    def step(act_out, act_in):
        for r in range(rows_per_step):
            i0 = blk * rows_per_step + r
            lrows = [lrow_ref[h, pl.ds(i0, 1), :] for h in range(PEER_HEADS)]
            e0rows = [e0_ref[h, pl.ds(i0, 1), :] for h in range(PEER_HEADS)]
            for tc in range(tb // LANES):
                lanes = slice(tc * LANES, (tc + 1) * LANES)
                l_b = [jnp.broadcast_to(lrows[h][:, lanes], (pack, LANES)).astype(BF16)
                       for h in range(PEER_HEADS)]
                e0_b = [jnp.broadcast_to(e0rows[h][:, lanes], (pack, LANES)).astype(BF16)
                        for h in range(PEER_HEADS)]
                for ib in range(N_KEYS // pack):
                    wrows = slice(ib * SUBLANES, (ib + 1) * SUBLANES)
                    gate = None
                    for h in range(PEER_HEADS):
                        rank = pltpu.bitcast(r1_ref[h, wrows, lanes], BF16)
                        e1 = pltpu.bitcast(e1_ref[h, wrows, lanes], BF16)
                        term = jnp.where(rank < l_b[h], e1, zero) * e0_b[h]
                        gate = term if gate is None else gate + term
                    rows = slice(r * N_KEYS + ib * pack, r * N_KEYS + (ib + 1) * pack)
                    w_ref[rows, lanes] = gate * _gelu_tanh_x2(act_in[rows, lanes]).astype(BF16)
        ot_ref[...] += _dot(vt_ref[...], w_ref[...])
        act_out[...] = _dot(u_ref[...], xnt_ref[...])

    @pl.when(e % 2 == 0)
    def _():
        step(act0_ref, act1_ref)

    @pl.when(e % 2 == 1)
    def _():
        step(act1_ref, act0_ref)


def _peer_dense(xnt, u16, vt16, lrow, e0, r1, e1, tb, eb):
    d, m = xnt.shape
    n_blocks = u16.shape[0] // eb
    slab_spec = pl.BlockSpec((PEER_HEADS, N_KEYS, tb), lambda i, e: (0, 0, i))
    pack_spec = pl.BlockSpec((PEER_HEADS, N_KEYS // 2, tb), lambda i, e: (0, 0, i))
    return pl.pallas_call(
        functools.partial(_peer_dense_kernel, rows_per_step=eb // N_KEYS, n_blocks=n_blocks),
        grid=(m // tb, n_blocks + 1),
        in_specs=[pl.BlockSpec((d, tb), lambda i, e: (0, i)),
                  pl.BlockSpec((eb, d), lambda i, e: (jnp.minimum(e, n_blocks - 1), 0)),
                  pl.BlockSpec((d, eb), lambda i, e: (0, jnp.clip(e - 1, 0, n_blocks - 1))),
                  slab_spec, slab_spec, pack_spec, pack_spec],
        out_specs=pl.BlockSpec((d, tb), lambda i, e: (0, i)),
        out_shape=jax.ShapeDtypeStruct((d, m), F32),
        scratch_shapes=[pltpu.VMEM((eb, tb), F32), pltpu.VMEM((eb, tb), F32), pltpu.VMEM((eb, tb), BF16)],
        compiler_params=_cparams(("parallel", "arbitrary")),
        name="peer_dense",
    )(xnt, u16, vt16, lrow, e0, r1, e1)


def _ple_kernel(h_ref, po_ref, g_ref, wg_ref, p_ref, wu_ref, fg_ref, o_ref):
    h = h_ref[...] + po_ref[...].T
    xn = _rms(h, g_ref[...]).astype(BF16)
    gate = _sigmoid(_dot(xn, wg_ref[...]))
    up = _dot(p_ref[...].astype(BF16), wu_ref[...])
    o_ref[...] = _rms(h + gate * up, fg_ref[...])


def _ple_final(h, peer_out, g, w_gate, p, w_up, final_g, tm):
    m, d = h.shape
    pd = p.shape[1]
    return pl.pallas_call(
        _ple_kernel,
        grid=(m // tm,),
        in_specs=[pl.BlockSpec((tm, d), lambda i: (i, 0)),
                  pl.BlockSpec((d, tm), lambda i: (0, i)),
                  pl.BlockSpec((1, d), lambda i: (0, 0)),
                  pl.BlockSpec((d, d), lambda i: (0, 0)),
                  pl.BlockSpec((tm, pd), lambda i: (i, 0)),
                  pl.BlockSpec((pd, d), lambda i: (0, 0)),
                  pl.BlockSpec((1, d), lambda i: (0, 0))],
        out_specs=pl.BlockSpec((tm, d), lambda i: (i, 0)),
        out_shape=jax.ShapeDtypeStruct((m, d), F32),
        compiler_params=_cparams(("parallel",)),
        name="ple_final",
    )(h, peer_out, g.reshape(1, d), w_gate.astype(BF16), p, w_up.astype(BF16),
      final_g.reshape(1, d))


def _pick(n, prefs):
    for t in prefs:
        if n % t == 0:
            return t
    return n


def kernel(x, p, mix_norm_g, w_in, rwkv_mu, decay_up, decay_w0, iclr_up, iclr_a0, gate_up, k_k, k_a, r_k, lnx_g, lnx_b, rel_bias, w_out, ffn_norm_g, peer_wq, peer_sub_keys, peer_u, peer_v, ple_norm_g, ple_gate_w, ple_up, final_norm_g):
    batch, seq, d = x.shape
    m = batch * seq
    assert w_in.shape[0] == 1, "the final norm is fused into the (single) layer's last stage"
    rwkv_w = decay_up.shape[2]
    rwkv_cols = rwkv_mu.shape[1]
    h = x.reshape(m, d)
    for i in range(1):
        w16 = w_in[i].astype(BF16)
        tm = _pick(m, (1024, 512, 256, 128))
        z_rwkv = _norm_matmul(h, mix_norm_g[i], w16[:, :rwkv_cols], F32, tm,
                              _pick(rwkv_cols, (1664, 1024, 512, 256, 128)))
        z_att = _norm_matmul(h, mix_norm_g[i], w16[:, rwkv_cols:], BF16, tm,
                             _pick(w16.shape[1] - rwkv_cols, (1536, 1024, 512, 256, 128)))
        y_rwkv = _rwkv(z_rwkv, batch, seq, rwkv_mu[i], decay_up[i], decay_w0[i], iclr_up[i],
                       iclr_a0[i], gate_up[i], k_k[i], k_a[i], r_k[i].reshape(rwkv_w),
                       lnx_g[i], lnx_b[i], _pick(seq, (512, 256, 128, 64)))
        y_att = _attention(z_att, batch, seq, rel_bias[i], _pick(seq, (512,)))
        h1 = _out_proj(h, y_rwkv, y_att, w_out[i], _pick(m, (512, 256, 128)))
        q, xnt = _peer_query(h1, ffn_norm_g[i], peer_wq[i], _pick(m, (512, 256, 128)))
        lrow, e0, r1, e1 = _peer_scores(q, peer_sub_keys[i], _pick(m, (256, 128)))
        peer_out_t = _peer_dense(xnt, peer_u[i].astype(BF16), peer_v[i].T.astype(BF16), lrow, e0, r1,
                                 e1, _pick(m, (512, 256, 128)), 8 * N_KEYS)
        h = _ple_final(h1, peer_out_t, ple_norm_g[i], ple_gate_w[i], p[i].reshape(m, -1), ple_up[i],
                       final_norm_g, _pick(m, (512, 256, 128)))
    return h.reshape(batch, seq, d)
```
